```python
import math
import jax, jax.numpy as jnp
from jax import lax
import numpy as np

D_MODEL = 2048
BATCH = 4
SEQ = 4096
DEPTH = 1

N_ATTN_HEADS = 8
ATTN_QK_DIM = 64
ATTN_V_DIM = 2 * ATTN_QK_DIM
ATTN_QK_WIDTH = N_ATTN_HEADS * 2 * ATTN_QK_DIM
ATTN_WIDTH = N_ATTN_HEADS * ATTN_V_DIM
Q_BLOCK = 128
SG_WIDTH = D_MODEL // 2
N_SG_GROUPS = 8
SG_GROUP_DIM = SG_WIDTH // N_SG_GROUPS
SG_CHUNK = 128
N_BRANCHES = 2
D_FF = 4 * D_MODEL
Q_OFF = 0
K_OFF = Q_OFF + ATTN_QK_WIDTH
V_OFF = K_OFF + ATTN_QK_WIDTH
U_OFF = V_OFF + ATTN_WIDTH
SGV_OFF = U_OFF + SG_WIDTH
GATE_OFF = SGV_OFF + SG_WIDTH
IN_WIDTH = GATE_OFF + N_BRANCHES * D_MODEL
DEEPNORM_ALPHA = (2.0 * DEPTH) ** 0.25
DEEPNORM_BETA = (8.0 * DEPTH) ** -0.25
LN_EPS = 1e-5

kernel_name = "hybrid_diffattn_sgu_gated_deepnorm_block"


def layer_norm(x, g, b):
    xf = x.astype(jnp.float32)
    mu = jnp.mean(xf, axis=-1, keepdims=True)
    var = jnp.mean(jnp.square(xf - mu), axis=-1, keepdims=True)
    y = (xf - mu) * lax.rsqrt(var + LN_EPS) * g.astype(jnp.float32) + b.astype(jnp.float32)
    return y.astype(x.dtype)


def alibi_slopes(n_heads):
    i = jnp.arange(1, n_heads + 1, dtype=jnp.float32)
    return jnp.exp2(-8.0 * i / n_heads)


def diff_attention(q, k, v, lam, lambda_init, sub_gain):
    B, S, H, _, dk = q.shape
    nb = S // Q_BLOCK
    slopes = alibi_slopes(H)
    k_pos = jnp.arange(S, dtype=jnp.float32)
    scale = dk ** -0.5
    q_blocks = q.reshape(B, nb, Q_BLOCK, H, 2, dk).transpose(1, 0, 2, 3, 4, 5)

    def one_block(args):
        q_blk, i = args
        q_pos = (i * Q_BLOCK + jnp.arange(Q_BLOCK)).astype(jnp.float32)
        dist = jnp.abs(q_pos[:, None] - k_pos[None, :])
        bias = -slopes[:, None, None] * dist[None]
        s = jnp.einsum('bqhcd,bkhcd->bhcqk', q_blk, k,
                       preferred_element_type=jnp.float32) * scale
        p = jax.nn.softmax(s + bias[None, :, None], axis=-1)
        w = p[:, :, 0] - lam * p[:, :, 1]
        return jnp.einsum('bhqk,bkhd->bqhd', w.astype(v.dtype), v)

    o = lax.map(one_block, (q_blocks, jnp.arange(nb)))
    o = o.transpose(1, 0, 2, 3, 4).reshape(B, S, H, v.shape[-1])
    of = o.astype(jnp.float32)
    of = of * lax.rsqrt(jnp.mean(jnp.square(of), axis=-1, keepdims=True) + LN_EPS)
    of = of * sub_gain.astype(jnp.float32) * (1.0 - lambda_init)
    return of.astype(v.dtype).reshape(B, S, H * v.shape[-1])


def spatial_gating(u, v, ln_g, ln_b, w_s, b_s):
    B, S, _ = v.shape
    v = layer_norm(v, ln_g, ln_b)
    vc = v.reshape(B, S // SG_CHUNK, SG_CHUNK, N_SG_GROUPS, SG_GROUP_DIM)
    mixed = jnp.einsum('gts,bcsgd->bctgd', w_s, vc) + b_s.T[:, :, None]
    return u * mixed.reshape(B, S, SG_WIDTH)


def setup_inputs(seed: int = 0) -> dict:
    key = jax.random.key(seed)
    ks = jax.random.split(key, 24)
    f32 = jnp.float32
    L, D = DEPTH, D_MODEL

    def nrm(k, shape, scale):
        return jax.random.normal(k, shape, f32) * scale

    return {
        "x": jax.random.normal(ks[0], (BATCH, SEQ, D), f32),
        "w_in": nrm(ks[1], (L, D, IN_WIDTH), D ** -0.5),
        "lambda_q1": nrm(ks[2], (L, ATTN_QK_DIM), 0.1),
        "lambda_k1": nrm(ks[3], (L, ATTN_QK_DIM), 0.1),
        "lambda_q2": nrm(ks[4], (L, ATTN_QK_DIM), 0.1),
        "lambda_k2": nrm(ks[5], (L, ATTN_QK_DIM), 0.1),
        "attn_subln_g": 1.0 + nrm(ks[6], (L, ATTN_V_DIM), 0.02),
        "sg_ln_g": 1.0 + nrm(ks[7], (L, SG_WIDTH), 0.02),
        "sg_ln_b": nrm(ks[8], (L, SG_WIDTH), 0.02),
        "sg_w_s": nrm(ks[9], (L, N_SG_GROUPS, SG_CHUNK, SG_CHUNK), SG_CHUNK ** -0.5),
        "sg_b_s": 1.0 + nrm(ks[10], (L, N_SG_GROUPS, SG_CHUNK), 0.02),
        "b_gate": nrm(ks[11], (L, N_BRANCHES * D), 0.02),
        "w_branch_attn": nrm(ks[12], (L, ATTN_WIDTH, D), ATTN_WIDTH ** -0.5),
        "w_branch_sg": nrm(ks[13], (L, SG_WIDTH, D), SG_WIDTH ** -0.5),
        "w_o": nrm(ks[14], (L, D, D), D ** -0.5 * DEEPNORM_BETA),
        "ln1_g": 1.0 + nrm(ks[15], (L, D), 0.02),
        "ln1_b": nrm(ks[16], (L, D), 0.02),
        "w_up": nrm(ks[17], (L, D, D_FF), D ** -0.5),
        "w_down": nrm(ks[18], (L, D_FF, D), D_FF ** -0.5 * DEEPNORM_BETA),
        "ln2_g": 1.0 + nrm(ks[19], (L, D), 0.02),
        "ln2_b": nrm(ks[20], (L, D), 0.02),
    }


def reference(x, w_in, lambda_q1, lambda_k1, lambda_q2, lambda_k2, attn_subln_g,
              sg_ln_g, sg_ln_b, sg_w_s, sg_b_s, b_gate, w_branch_attn, w_branch_sg,
              w_o, ln1_g, ln1_b, w_up, w_down, ln2_g, ln2_b):
    B, S, D = x.shape
    f32 = jnp.float32
    for l in range(DEPTH):
        lambda_init = 0.8 - 0.6 * math.exp(-0.3 * l)
        lam = (jnp.exp(jnp.sum(lambda_q1[l].astype(f32) * lambda_k1[l].astype(f32)))
               - jnp.exp(jnp.sum(lambda_q2[l].astype(f32) * lambda_k2[l].astype(f32)))
               + lambda_init)

        proj = x @ w_in[l]
        q = proj[..., Q_OFF:K_OFF].reshape(B, S, N_ATTN_HEADS, 2, ATTN_QK_DIM)
        k = proj[..., K_OFF:V_OFF].reshape(B, S, N_ATTN_HEADS, 2, ATTN_QK_DIM)
        v = proj[..., V_OFF:U_OFF].reshape(B, S, N_ATTN_HEADS, ATTN_V_DIM)
        sg = jax.nn.gelu(proj[..., U_OFF:GATE_OFF], approximate=False)
        sg_u, sg_v = sg[..., :SG_WIDTH], sg[..., SG_WIDTH:]
        gates = jax.nn.sigmoid(proj[..., GATE_OFF:] + b_gate[l]).reshape(B, S, N_BRANCHES, D)

        attn_out = diff_attention(q, k, v, lam, lambda_init, attn_subln_g[l])
        sg_out = spatial_gating(sg_u, sg_v, sg_ln_g[l], sg_ln_b[l], sg_w_s[l], sg_b_s[l])

        merged = (gates[:, :, 0] * (attn_out @ w_branch_attn[l])
                  + gates[:, :, 1] * (sg_out @ w_branch_sg[l]))
        x = layer_norm(DEEPNORM_ALPHA * x + merged @ w_o[l], ln1_g[l], ln1_b[l])

        h = jnp.square(jax.nn.relu(x @ w_up[l]))
        x = layer_norm(DEEPNORM_ALPHA * x + h @ w_down[l], ln2_g[l], ln2_b[l])
    return x
```

```python
import functools
import math

import jax
import jax.numpy as jnp
from jax import lax
from jax.experimental import pallas as pl
from jax.experimental.pallas import tpu as pltpu

F32 = jnp.float32
BF16 = jnp.bfloat16

N_HEADS = 8
QK_DIM = 64
HEAD_DIM = 2 * QK_DIM
ATTN_WIDTH = N_HEADS * HEAD_DIM
SG_WIDTH = 1024
SG_GROUPS = 8
SG_CHUNK = 128
LN_EPS = 1e-5
INV_SQRT2 = 0.7071067811865476

VMEM_LIMIT_BYTES = 60 * 1024 * 1024


def _params(*sem):
    return pltpu.CompilerParams(dimension_semantics=sem, vmem_limit_bytes=VMEM_LIMIT_BYTES)


def _layer_norm(y, g, b):
    mu = jnp.mean(y, axis=-1, keepdims=True)
    d = y - mu
    var = jnp.mean(d * d, axis=-1, keepdims=True)
    return d * lax.rsqrt(var + LN_EPS) * g + b


def _proj_kernel(x_ref, w_ref, o_ref, *, gelu):
    y = jnp.dot(x_ref[...], w_ref[...], preferred_element_type=F32)
    if gelu:
        y = 0.5 * y * (1.0 + lax.erf(y * INV_SQRT2))
    o_ref[...] = y.astype(o_ref.dtype)


def _proj(x, w, col0, ncols, *, gelu, out_dtype, tm=1024, tn=1024):
    m, k = x.shape
    j0 = col0 // tn
    return pl.pallas_call(
        functools.partial(_proj_kernel, gelu=gelu),
        grid=(m // tm, ncols // tn),
        in_specs=[pl.BlockSpec((tm, k), lambda i, j: (i, 0)),
                  pl.BlockSpec((k, tn), lambda i, j: (0, j0 + j))],
        out_specs=pl.BlockSpec((tm, tn), lambda i, j: (i, j)),
        out_shape=jax.ShapeDtypeStruct((m, ncols), out_dtype),
        compiler_params=_params("parallel", "arbitrary"),
    )(x, w)


def _sgu_kernel(u_ref, v_ref, g_ref, b_ref, ws_ref, bs_ref, o_ref, vn_ref):
    vn_ref[...] = _layer_norm(v_ref[...], g_ref[...], b_ref[...]).astype(BF16)
    tm = u_ref.shape[0]
    gd = SG_WIDTH // SG_GROUPS
    for c in range(tm // SG_CHUNK):
        rows = slice(c * SG_CHUNK, (c + 1) * SG_CHUNK)
        for g in range(SG_GROUPS):
            cols = slice(g * gd, (g + 1) * gd)
            mixed = jnp.dot(ws_ref[g], vn_ref[rows, cols], preferred_element_type=F32)
            mixed = mixed + bs_ref[:, cols]
            o_ref[rows, cols] = (u_ref[rows, cols] * mixed).astype(o_ref.dtype)


def _sgu(sg, ln_g, ln_b, w_s, bs_full, *, tm=512):
    m = sg.shape[0]
    return pl.pallas_call(
        _sgu_kernel,
        grid=(m // tm,),
        in_specs=[pl.BlockSpec((tm, SG_WIDTH), lambda i: (i, 0)),
                  pl.BlockSpec((tm, SG_WIDTH), lambda i: (i, 1)),
                  pl.BlockSpec((1, SG_WIDTH), lambda i: (0, 0)),
                  pl.BlockSpec((1, SG_WIDTH), lambda i: (0, 0)),
                  pl.BlockSpec(w_s.shape, lambda i: (0, 0, 0)),
                  pl.BlockSpec(bs_full.shape, lambda i: (0, 0))],
        out_specs=pl.BlockSpec((tm, SG_WIDTH), lambda i: (i, 0)),
        out_shape=jax.ShapeDtypeStruct((m, SG_WIDTH), BF16),
        scratch_shapes=[pltpu.VMEM((tm, SG_WIDTH), BF16)],
        compiler_params=_params("parallel"),
    )(sg, sg, ln_g, ln_b, w_s, bs_full)


def _attn_kernel(slopes_ref, lq1_ref, lk1_ref, lq2_ref, lk2_ref, gain_ref,
                 q_ref, k_ref, v_ref, o_ref,
                 wq_ref, tpos_ref, tdiag_ref, m_ref, l_ref, acc_ref,
                 *, t, lambda_init):
    h = pl.program_id(1)
    qi = pl.program_id(2)
    nk = k_ref.shape[1] // t
    slope = slopes_ref[h]

    @pl.when(qi == 0)
    def _fill_bias():
        key = lax.broadcasted_iota(jnp.int32, (t, t), 0)
        qry = lax.broadcasted_iota(jnp.int32, (t, t), 1)
        d = (qry - key).astype(F32)
        tpos_ref[...] = -slope * d
        tdiag_ref[...] = -slope * jnp.abs(d)

    q = q_ref[0].astype(F32)
    lane = lax.broadcasted_iota(jnp.int32, q.shape, 1)
    wq_ref[0:t, :] = jnp.where(lane < QK_DIM, q, 0.0).astype(BF16)
    wq_ref[t:2 * t, :] = jnp.where(lane >= QK_DIM, q, 0.0).astype(BF16)

    def block(j, bias, sign, shift, first):
        start = pl.multiple_of(j * t, t)
        kt = k_ref[0, pl.ds(start, t), :]
        vt = v_ref[0, pl.ds(start, t), :]
        s = lax.dot_general(kt, wq_ref[...], (((1,), (1,)), ((), ())),
                            preferred_element_type=F32)
        for c in range(2):
            sc = s[:, c * t:(c + 1) * t]
            z = sc + bias if sign > 0 else sc - bias
            mb = jnp.max(z, axis=0, keepdims=True) + shift
            if first:
                m_new = mb
            else:
                m_old = m_ref[c]
                m_new = jnp.maximum(m_old, mb)
                alpha = jnp.exp(m_old - m_new)
            p = jnp.exp(z - (m_new - shift))
            ps = jnp.sum(p, axis=0, keepdims=True)
            pv = lax.dot_general(vt, p.astype(BF16), (((0,), (0,)), ((), ())),
                                 preferred_element_type=F32)
            if first:
                l_ref[c] = ps
                acc_ref[c] = pv
            else:
                l_ref[c] = alpha * l_ref[c] + ps
                acc_ref[c] = alpha * acc_ref[c] + pv
            m_ref[c] = m_new

    block(qi, tdiag_ref[...], 1, 0.0, True)

    def before(j, carry):
        block(j, tpos_ref[...], 1, -slope * ((qi - j) * t).astype(F32), False)
        return carry

    def after(j, carry):
        block(j, tpos_ref[...], -1, -slope * ((j - qi) * t).astype(F32), False)
        return carry

    lax.fori_loop(0, qi, before, 0)
    lax.fori_loop(qi + 1, nk, after, 0)

    lam = (jnp.exp(jnp.sum(lq1_ref[...] * lk1_ref[...], axis=-1, keepdims=True))
           - jnp.exp(jnp.sum(lq2_ref[...] * lk2_ref[...], axis=-1, keepdims=True))
           + lambda_init)
    o = acc_ref[0] / l_ref[0] - lam * (acc_ref[1] / l_ref[1])
    ms = jnp.mean(o * o, axis=0, keepdims=True)
    o = o * lax.rsqrt(ms + LN_EPS)
    o = o.T * gain_ref[...] * (1.0 - lambda_init)
    o_ref[0] = o.astype(o_ref.dtype)


def _attention(qkv, slopes, lq1, lk1, lq2, lk2, gain, *, batch, seq, lambda_init, t=512):
    nh = N_HEADS
    vec = lambda n: pl.BlockSpec((1, n), lambda b, h, i: (0, 0))
    return pl.pallas_call(
        functools.partial(_attn_kernel, t=t, lambda_init=lambda_init),
        grid=(batch, nh, seq // t),
        in_specs=[pl.BlockSpec(memory_space=pltpu.SMEM),
                  vec(QK_DIM), vec(QK_DIM), vec(QK_DIM), vec(QK_DIM), vec(HEAD_DIM),
                  pl.BlockSpec((1, t, HEAD_DIM), lambda b, h, i: (b, i, h)),
                  pl.BlockSpec((1, seq, HEAD_DIM), lambda b, h, i: (b, 0, nh + h)),
                  pl.BlockSpec((1, seq, HEAD_DIM), lambda b, h, i: (b, 0, 2 * nh + h))],
        out_specs=pl.BlockSpec((1, t, HEAD_DIM), lambda b, h, i: (b, i, h)),
        out_shape=jax.ShapeDtypeStruct((batch, seq, ATTN_WIDTH), BF16),
        scratch_shapes=[pltpu.VMEM((2 * t, HEAD_DIM), BF16),
                        pltpu.VMEM((t, t), F32),
                        pltpu.VMEM((t, t), F32),
                        pltpu.VMEM((2, 1, t), F32),
                        pltpu.VMEM((2, 1, t), F32),
                        pltpu.VMEM((2, HEAD_DIM, t), F32)],
        compiler_params=_params("parallel", "arbitrary", "arbitrary"),
    )(slopes, lq1, lk1, lq2, lk2, gain, qkv, qkv, qkv)


def _merge_kernel(a_ref, s_ref, x_ref, wa_ref, ws_ref, wg0_ref, wg1_ref, bg0_ref, bg1_ref, o_ref):
    x = x_ref[...]
    g0 = jax.nn.sigmoid(jnp.dot(x, wg0_ref[...], preferred_element_type=F32) + bg0_ref[...])
    g1 = jax.nn.sigmoid(jnp.dot(x, wg1_ref[...], preferred_element_type=F32) + bg1_ref[...])
    ya = jnp.dot(a_ref[...], wa_ref[...], preferred_element_type=F32)
    ys = jnp.dot(s_ref[...], ws_ref[...], preferred_element_type=F32)
    o_ref[...] = (g0 * ya + g1 * ys).astype(o_ref.dtype)


def _merge(attn, sgo, x, w_in, gate_off, b_gate, w_ba, w_bs, *, tm=1024, tn=512):
    m, d = x.shape
    j0 = gate_off // tn
    j1 = (gate_off + d) // tn
    nb = d // tn
    return pl.pallas_call(
        _merge_kernel,
        grid=(m // tm, d // tn),
        in_specs=[pl.BlockSpec((tm, attn.shape[1]), lambda i, j: (i, 0)),
                  pl.BlockSpec((tm, sgo.shape[1]), lambda i, j: (i, 0)),
                  pl.BlockSpec((tm, d), lambda i, j: (i, 0)),
                  pl.BlockSpec((w_ba.shape[0], tn), lambda i, j: (0, j)),
                  pl.BlockSpec((w_bs.shape[0], tn), lambda i, j: (0, j)),
                  pl.BlockSpec((d, tn), lambda i, j: (0, j0 + j)),
                  pl.BlockSpec((d, tn), lambda i, j: (0, j1 + j)),
                  pl.BlockSpec((1, tn), lambda i, j: (0, j)),
                  pl.BlockSpec((1, tn), lambda i, j: (0, nb + j))],
        out_specs=pl.BlockSpec((tm, tn), lambda i, j: (i, j)),
        out_shape=jax.ShapeDtypeStruct((m, d), BF16),
        compiler_params=_params("parallel", "arbitrary"),
    )(attn, sgo, x, w_ba, w_bs, w_in, w_in, b_gate, b_gate)


def _oproj_kernel(y_ref, w_ref, x_ref, g_ref, b_ref, o_ref, *, alpha):
    y = jnp.dot(y_ref[...], w_ref[...], preferred_element_type=F32)
    o_ref[...] = _layer_norm(alpha * x_ref[...] + y, g_ref[...], b_ref[...])


def _oproj(merged, w_o, x, g, b, *, alpha, tm=512):
    m, d = x.shape
    return pl.pallas_call(
        functools.partial(_oproj_kernel, alpha=alpha),
        grid=(m // tm,),
        in_specs=[pl.BlockSpec((tm, d), lambda i: (i, 0)),
                  pl.BlockSpec((d, d), lambda i: (0, 0)),
                  pl.BlockSpec((tm, d), lambda i: (i, 0)),
                  pl.BlockSpec((1, d), lambda i: (0, 0)),
                  pl.BlockSpec((1, d), lambda i: (0, 0))],
        out_specs=pl.BlockSpec((tm, d), lambda i: (i, 0)),
        out_shape=jax.ShapeDtypeStruct((m, d), F32),
        compiler_params=_params("parallel"),
    )(merged, w_o, x, g, b)


def _mlp_kernel(x_ref, wu_ref, wd_ref, g_ref, b_ref, o_ref, xb_ref, *, alpha):
    f = pl.program_id(1)

    @pl.when(f == 0)
    def _():
        xb_ref[...] = x_ref[...].astype(BF16)

    hid = jnp.dot(xb_ref[...], wu_ref[...], preferred_element_type=F32)
    hid = jnp.square(jnp.maximum(hid, 0.0)).astype(BF16)
    y = jnp.dot(hid, wd_ref[...], preferred_element_type=F32)

    @pl.when(f == 0)
    def _():
        o_ref[...] = y

    @pl.when(f > 0)
    def _():
        o_ref[...] += y

    @pl.when(f == pl.num_programs(1) - 1)
    def _():
        o_ref[...] = _layer_norm(alpha * x_ref[...] + o_ref[...], g_ref[...], b_ref[...])


def _mlp(x, w_up, w_down, g, b, *, alpha, tm=512, tf=1024):
    m, d = x.shape
    dff = w_up.shape[1]
    return pl.pallas_call(
        functools.partial(_mlp_kernel, alpha=alpha),
        grid=(m // tm, dff // tf),
        in_specs=[pl.BlockSpec((tm, d), lambda i, f: (i, 0)),
                  pl.BlockSpec((d, tf), lambda i, f: (0, f)),
                  pl.BlockSpec((tf, d), lambda i, f: (f, 0)),
                  pl.BlockSpec((1, d), lambda i, f: (0, 0)),
                  pl.BlockSpec((1, d), lambda i, f: (0, 0))],
        out_specs=pl.BlockSpec((tm, d), lambda i, f: (i, 0)),
        out_shape=jax.ShapeDtypeStruct((m, d), F32),
        scratch_shapes=[pltpu.VMEM((tm, d), BF16)],
        compiler_params=_params("parallel", "arbitrary"),
    )(x, w_up, w_down, g, b)


def kernel(x, w_in, lambda_q1, lambda_k1, lambda_q2, lambda_k2, attn_subln_g, sg_ln_g, sg_ln_b, sg_w_s, sg_b_s, b_gate, w_branch_attn, w_branch_sg, w_o, ln1_g, ln1_b, w_up, w_down, ln2_g, ln2_b):
    batch, seq, d = x.shape
    depth = w_in.shape[0]
    m = batch * seq
    alpha = (2.0 * depth) ** 0.25
    k_off = ATTN_WIDTH
    u_off = 3 * ATTN_WIDTH
    gate_off = u_off + 2 * SG_WIDTH
    row = lambda p: p.reshape(1, -1).astype(F32)

    i = jnp.arange(1, N_HEADS + 1, dtype=F32)
    slopes = jnp.exp2(-8.0 * i / N_HEADS)
    col_scale = jnp.where(jnp.arange(w_in.shape[2]) < k_off, QK_DIM ** -0.5, 1.0).astype(F32)

    xf = x.reshape(m, d)
    for l in range(depth):
        lambda_init = 0.8 - 0.6 * math.exp(-0.3 * l)
        w_in_b = (w_in[l] * col_scale).astype(BF16)
        xb = xf.astype(BF16)

        qkv = _proj(xb, w_in_b, 0, 3 * ATTN_WIDTH, gelu=False, out_dtype=BF16)
        sg = _proj(xb, w_in_b, u_off, 2 * SG_WIDTH, gelu=True, out_dtype=F32)

        gd = SG_WIDTH // SG_GROUPS
        bs_full = jnp.repeat(sg_b_s[l].T.astype(F32), gd, axis=1)
        sgo = _sgu(sg, row(sg_ln_g[l]), row(sg_ln_b[l]), sg_w_s[l].astype(BF16), bs_full)

        attn = _attention(qkv.reshape(batch, seq, 3 * ATTN_WIDTH), slopes,
                          row(lambda_q1[l]), row(lambda_k1[l]), row(lambda_q2[l]), row(lambda_k2[l]),
                          row(attn_subln_g[l]), batch=batch, seq=seq, lambda_init=lambda_init)

        merged = _merge(attn.reshape(m, ATTN_WIDTH), sgo, xb, w_in_b, gate_off, row(b_gate[l]),
                        w_branch_attn[l].astype(BF16), w_branch_sg[l].astype(BF16))
        xf = _oproj(merged, w_o[l].astype(BF16), xf, row(ln1_g[l]), row(ln1_b[l]), alpha=alpha)
        xf = _mlp(xf, w_up[l].astype(BF16), w_down[l].astype(BF16), row(ln2_g[l]), row(ln2_b[l]), alpha=alpha)
    return xf.reshape(batch, seq, d)
```

```python
import functools
import math

import jax
import jax.numpy as jnp
from jax import lax
from jax.experimental import pallas as pl
from jax.experimental.pallas import tpu as pltpu

F32 = jnp.float32
BF16 = jnp.bfloat16

N_HEADS = 8
QK_DIM = 64
HEAD_DIM = 2 * QK_DIM
ATTN_WIDTH = N_HEADS * HEAD_DIM
SG_WIDTH = 1024
SG_GROUPS = 8
SG_CHUNK = 128
LN_EPS = 1e-5
INV_SQRT2 = 0.7071067811865476
LOG2E = 1.4426950408889634
LANES = 128

VMEM_LIMIT_BYTES = 60 * 1024 * 1024


def _params(*sem):
    return pltpu.CompilerParams(dimension_semantics=sem, vmem_limit_bytes=VMEM_LIMIT_BYTES)


def _layer_norm(y, g, b):
    mu = jnp.mean(y, axis=-1, keepdims=True)
    d = y - mu
    var = jnp.mean(d * d, axis=-1, keepdims=True)
    return d * lax.rsqrt(var + LN_EPS) * g + b


def _proj_kernel(x_ref, w_ref, o_ref, *, gelu):
    y = jnp.dot(x_ref[...], w_ref[...], preferred_element_type=F32)
    if gelu:
        y = 0.5 * y * (1.0 + lax.erf(y * INV_SQRT2))
    o_ref[...] = y.astype(o_ref.dtype)


def _proj(x, w, col0, ncols, *, gelu, out_dtype, tm=1024, tn=1024):
    m, k = x.shape
    j0 = col0 // tn
    return pl.pallas_call(
        functools.partial(_proj_kernel, gelu=gelu),
        grid=(m // tm, ncols // tn),
        in_specs=[pl.BlockSpec((tm, k), lambda i, j: (i, 0)),
                  pl.BlockSpec((k, tn), lambda i, j: (0, j0 + j))],
        out_specs=pl.BlockSpec((tm, tn), lambda i, j: (i, j)),
        out_shape=jax.ShapeDtypeStruct((m, ncols), out_dtype),
        compiler_params=_params("parallel", "arbitrary"),
    )(x, w)


def _sgu_kernel(u_ref, v_ref, g_ref, b_ref, ws_ref, bs_ref, o_ref, vn_ref):
    vn_ref[...] = _layer_norm(v_ref[...], g_ref[...], b_ref[...]).astype(BF16)
    tm = u_ref.shape[0]
    gd = SG_WIDTH // SG_GROUPS
    for c in range(tm // SG_CHUNK):
        rows = slice(c * SG_CHUNK, (c + 1) * SG_CHUNK)
        for g in range(SG_GROUPS):
            cols = slice(g * gd, (g + 1) * gd)
            mixed = jnp.dot(ws_ref[g], vn_ref[rows, cols], preferred_element_type=F32)
            mixed = mixed + bs_ref[:, cols]
            o_ref[rows, cols] = (u_ref[rows, cols] * mixed).astype(o_ref.dtype)


def _sgu(sg, ln_g, ln_b, w_s, bs_full, *, tm=512):
    m = sg.shape[0]
    return pl.pallas_call(
        _sgu_kernel,
        grid=(m // tm,),
        in_specs=[pl.BlockSpec((tm, SG_WIDTH), lambda i: (i, 0)),
                  pl.BlockSpec((tm, SG_WIDTH), lambda i: (i, 1)),
                  pl.BlockSpec((1, SG_WIDTH), lambda i: (0, 0)),
                  pl.BlockSpec((1, SG_WIDTH), lambda i: (0, 0)),
                  pl.BlockSpec(w_s.shape, lambda i: (0, 0, 0)),
                  pl.BlockSpec(bs_full.shape, lambda i: (0, 0))],
        out_specs=pl.BlockSpec((tm, SG_WIDTH), lambda i: (i, 0)),
        out_shape=jax.ShapeDtypeStruct((m, SG_WIDTH), BF16),
        scratch_shapes=[pltpu.VMEM((tm, SG_WIDTH), BF16)],
        compiler_params=_params("parallel"),
    )(sg, sg, ln_g, ln_b, w_s, bs_full)


FEAT_ROWS = 8
ONES_ROWS = 16


def _feature_table(c):
    c1 = c.astype(BF16).astype(F32)
    c2 = (c - c1).astype(BF16).astype(F32)
    c3 = (c - c1 - c2).astype(BF16).astype(F32)
    nh = c.shape[0]
    z = jnp.zeros((nh, LANES), F32)
    ind = lambda lo: jnp.zeros((nh, LANES), F32).at[:, lo:lo + 3].set(1.0)
    pieces = jnp.stack([2 * c1, 2 * c2, 2 * c3, c1, c2, c3], axis=1)
    key_const = z.at[:, 6:12].set(pieces)
    qry_const = z.at[:, 0:6].set(pieces)
    return jnp.stack([ind(0), ind(3), key_const, qry_const, ind(6), ind(9), z, z], axis=1)


def _attn_kernel(c_ref, lq1_ref, lk1_ref, lq2_ref, lk2_ref, gain_ref, tab_ref,
                 q_ref, k_ref, v_ref, o_ref,
                 kf_ref, vt_ref, wq_ref, tcorr_ref, s_ref, p_ref, acc_ref,
                 *, t, lambda_init):
    h = pl.program_id(1)
    qi = pl.program_id(2)
    nk = k_ref.shape[1] // t
    c = c_ref[h]

    @pl.when(qi == 0)
    def _per_head():
        pos = lax.broadcasted_iota(jnp.int32, (t, LANES), 0)
        hi = (pos >> 1).astype(F32)
        lo = (pos & 1).astype(F32)
        tab = tab_ref[0]
        kfeat = (hi * tab[0:1] + lo * tab[1:2] + tab[2:3]).astype(BF16)
        qfeat = tab[3:4] - hi * tab[4:5] - lo * tab[5:6]
        kf_ref[:, 0:LANES] = k_ref[0]
        for j in range(nk):
            kf_ref[j * t:(j + 1) * t, LANES:2 * LANES] = kfeat
            vt_ref[j, 0:HEAD_DIM, :] = v_ref[0, j * t:(j + 1) * t, :].astype(F32).T.astype(BF16)
            vt_ref[j, HEAD_DIM:, :] = jnp.ones((ONES_ROWS, t), BF16)
        for half in range(2):
            rows = slice(half * t, (half + 1) * t)
            wq_ref[0, rows, LANES:2 * LANES] = qfeat.astype(BF16)
            wq_ref[1, rows, LANES:2 * LANES] = (-qfeat).astype(BF16)
        key = lax.broadcasted_iota(jnp.int32, (t, t), 0)
        qry = lax.broadcasted_iota(jnp.int32, (t, t), 1)
        tcorr_ref[...] = (-2.0 * c) * jnp.maximum(key - qry, 0).astype(F32)

    q = q_ref[0].astype(F32)
    lane = lax.broadcasted_iota(jnp.int32, q.shape, 1)
    q1 = jnp.where(lane < QK_DIM, q, 0.0).astype(BF16)
    q2 = jnp.where(lane >= QK_DIM, q, 0.0).astype(BF16)
    for sgn in range(2):
        wq_ref[sgn, 0:t, 0:LANES] = q1
        wq_ref[sgn, t:2 * t, 0:LANES] = q2

    def place(r):
        if r == 0:
            return qi, 0, 0
        j = qi + r
        after = j < nk
        return jnp.where(after, j, j - nk), jnp.where(after, 1, 0), jnp.where(after, r, nk - r)

    def scores(r):
        jw, sel, _ = place(r)
        kt = kf_ref[pl.ds(pl.multiple_of(jw * t, t), t), :]
        s = lax.dot_general(kt, wq_ref[sel], (((1,), (1,)), ((), ())),
                            preferred_element_type=F32)
        if r == 0:
            corr = tcorr_ref[...]
            s = jnp.concatenate([s[:, 0:t] + corr, s[:, t:2 * t] + corr], axis=1)
        s_ref[r % 2] = s
        return jnp.max(s, axis=0, keepdims=True)

    m = None
    mb = scores(0)
    for r in range(nk):
        mb_next = scores(r + 1) if r + 1 < nk else None
        jw, _, dist = place(r)
        shift = 0.0 if r == 0 else -c * (dist * t).astype(F32)
        mbs = mb + shift
        if r == 0:
            m_new = mbs
        else:
            m_new = jnp.maximum(m, mbs)
            alpha = jnp.exp2(m - m_new)
        p_ref[r % 2] = jnp.exp2(s_ref[r % 2] - (m_new - shift)).astype(BF16)
        pv = jnp.dot(vt_ref[jw], p_ref[r % 2], preferred_element_type=F32)
        if r == 0:
            acc_ref[...] = pv
        else:
            acc_ref[...] = alpha * acc_ref[...] + pv
        m = m_new
        mb = mb_next

    lam = (jnp.exp(jnp.sum(lq1_ref[...] * lk1_ref[...], axis=-1, keepdims=True))
           - jnp.exp(jnp.sum(lq2_ref[...] * lk2_ref[...], axis=-1, keepdims=True))
           + lambda_init)
    l = acc_ref[HEAD_DIM:HEAD_DIM + 1, :]
    o = (acc_ref[0:HEAD_DIM, 0:t] / l[:, 0:t]
         - lam * (acc_ref[0:HEAD_DIM, t:2 * t] / l[:, t:2 * t]))
    ms = jnp.mean(o * o, axis=0, keepdims=True)
    o = o * lax.rsqrt(ms + LN_EPS)
    o = o.T * gain_ref[...] * (1.0 - lambda_init)
    o_ref[0] = o.astype(o_ref.dtype)


def _attention(qkv, c, lq1, lk1, lq2, lk2, gain, *, batch, seq, lambda_init, t=512):
    nh = N_HEADS
    vec = lambda n: pl.BlockSpec((1, n), lambda b, h, i: (0, 0))
    return pl.pallas_call(
        functools.partial(_attn_kernel, t=t, lambda_init=lambda_init),
        grid=(batch, nh, seq // t),
        in_specs=[pl.BlockSpec(memory_space=pltpu.SMEM),
                  vec(QK_DIM), vec(QK_DIM), vec(QK_DIM), vec(QK_DIM), vec(HEAD_DIM),
                  pl.BlockSpec((1, FEAT_ROWS, LANES), lambda b, h, i: (h, 0, 0)),
                  pl.BlockSpec((1, t, HEAD_DIM), lambda b, h, i: (b, i, h)),
                  pl.BlockSpec((1, seq, HEAD_DIM), lambda b, h, i: (b, 0, nh + h)),
                  pl.BlockSpec((1, seq, HEAD_DIM), lambda b, h, i: (b, 0, 2 * nh + h))],
        out_specs=pl.BlockSpec((1, t, HEAD_DIM), lambda b, h, i: (b, i, h)),
        out_shape=jax.ShapeDtypeStruct((batch, seq, ATTN_WIDTH), BF16),
        scratch_shapes=[pltpu.VMEM((seq, 2 * LANES), BF16),
                        pltpu.VMEM((seq // t, HEAD_DIM + ONES_ROWS, t), BF16),
                        pltpu.VMEM((2, 2 * t, 2 * LANES), BF16),
                        pltpu.VMEM((t, t), F32),
                        pltpu.VMEM((2, t, 2 * t), F32),
                        pltpu.VMEM((2, t, 2 * t), BF16),
                        pltpu.VMEM((HEAD_DIM + ONES_ROWS, 2 * t), F32)],
        compiler_params=_params("parallel", "arbitrary", "arbitrary"),
    )(c, lq1, lk1, lq2, lk2, gain, _feature_table(c), qkv, qkv, qkv)


def _merge_kernel(a_ref, s_ref, x_ref, wa_ref, ws_ref, wg0_ref, wg1_ref, bg0_ref, bg1_ref, o_ref):
    x = x_ref[...]
    g0 = jax.nn.sigmoid(jnp.dot(x, wg0_ref[...], preferred_element_type=F32) + bg0_ref[...])
    g1 = jax.nn.sigmoid(jnp.dot(x, wg1_ref[...], preferred_element_type=F32) + bg1_ref[...])
    ya = jnp.dot(a_ref[...], wa_ref[...], preferred_element_type=F32)
    ys = jnp.dot(s_ref[...], ws_ref[...], preferred_element_type=F32)
    o_ref[...] = (g0 * ya + g1 * ys).astype(o_ref.dtype)


def _merge(attn, sgo, x, w_in, gate_off, b_gate, w_ba, w_bs, *, tm=1024, tn=512):
    m, d = x.shape
    j0 = gate_off // tn
    j1 = (gate_off + d) // tn
    nb = d // tn
    return pl.pallas_call(
        _merge_kernel,
        grid=(m // tm, d // tn),
        in_specs=[pl.BlockSpec((tm, attn.shape[1]), lambda i, j: (i, 0)),
                  pl.BlockSpec((tm, sgo.shape[1]), lambda i, j: (i, 0)),
                  pl.BlockSpec((tm, d), lambda i, j: (i, 0)),
                  pl.BlockSpec((w_ba.shape[0], tn), lambda i, j: (0, j)),
                  pl.BlockSpec((w_bs.shape[0], tn), lambda i, j: (0, j)),
                  pl.BlockSpec((d, tn), lambda i, j: (0, j0 + j)),
                  pl.BlockSpec((d, tn), lambda i, j: (0, j1 + j)),
                  pl.BlockSpec((1, tn), lambda i, j: (0, j)),
                  pl.BlockSpec((1, tn), lambda i, j: (0, nb + j))],
        out_specs=pl.BlockSpec((tm, tn), lambda i, j: (i, j)),
        out_shape=jax.ShapeDtypeStruct((m, d), BF16),
        compiler_params=_params("parallel", "arbitrary"),
    )(attn, sgo, x, w_ba, w_bs, w_in, w_in, b_gate, b_gate)


def _oproj_kernel(y_ref, w_ref, x_ref, g_ref, b_ref, o_ref, *, alpha):
    y = jnp.dot(y_ref[...], w_ref[...], preferred_element_type=F32)
    o_ref[...] = _layer_norm(alpha * x_ref[...] + y, g_ref[...], b_ref[...])


def _oproj(merged, w_o, x, g, b, *, alpha, tm=512):
    m, d = x.shape
    return pl.pallas_call(
        functools.partial(_oproj_kernel, alpha=alpha),
        grid=(m // tm,),
        in_specs=[pl.BlockSpec((tm, d), lambda i: (i, 0)),
                  pl.BlockSpec((d, d), lambda i: (0, 0)),
                  pl.BlockSpec((tm, d), lambda i: (i, 0)),
                  pl.BlockSpec((1, d), lambda i: (0, 0)),
                  pl.BlockSpec((1, d), lambda i: (0, 0))],
        out_specs=pl.BlockSpec((tm, d), lambda i: (i, 0)),
        out_shape=jax.ShapeDtypeStruct((m, d), F32),
        compiler_params=_params("parallel"),
    )(merged, w_o, x, g, b)


def _mlp_kernel(x_ref, wu_ref, wd_ref, g_ref, b_ref, o_ref, xb_ref, *, alpha):
    f = pl.program_id(1)

    @pl.when(f == 0)
    def _():
        xb_ref[...] = x_ref[...].astype(BF16)

    hid = jnp.dot(xb_ref[...], wu_ref[...], preferred_element_type=F32)
    hid = jnp.square(jnp.maximum(hid, 0.0)).astype(BF16)
    y = jnp.dot(hid, wd_ref[...], preferred_element_type=F32)

    @pl.when(f == 0)
    def _():
        o_ref[...] = y

    @pl.when(f > 0)
    def _():
        o_ref[...] += y

    @pl.when(f == pl.num_programs(1) - 1)
    def _():
        o_ref[...] = _layer_norm(alpha * x_ref[...] + o_ref[...], g_ref[...], b_ref[...])


def _mlp(x, w_up, w_down, g, b, *, alpha, tm=512, tf=1024):
    m, d = x.shape
    dff = w_up.shape[1]
    return pl.pallas_call(
        functools.partial(_mlp_kernel, alpha=alpha),
        grid=(m // tm, dff // tf),
        in_specs=[pl.BlockSpec((tm, d), lambda i, f: (i, 0)),
                  pl.BlockSpec((d, tf), lambda i, f: (0, f)),
                  pl.BlockSpec((tf, d), lambda i, f: (f, 0)),
                  pl.BlockSpec((1, d), lambda i, f: (0, 0)),
                  pl.BlockSpec((1, d), lambda i, f: (0, 0))],
        out_specs=pl.BlockSpec((tm, d), lambda i, f: (i, 0)),
        out_shape=jax.ShapeDtypeStruct((m, d), F32),
        scratch_shapes=[pltpu.VMEM((tm, d), BF16)],
        compiler_params=_params("parallel", "arbitrary"),
    )(x, w_up, w_down, g, b)


def kernel(x, w_in, lambda_q1, lambda_k1, lambda_q2, lambda_k2, attn_subln_g, sg_ln_g, sg_ln_b, sg_w_s, sg_b_s, b_gate, w_branch_attn, w_branch_sg, w_o, ln1_g, ln1_b, w_up, w_down, ln2_g, ln2_b):
    batch, seq, d = x.shape
    depth = w_in.shape[0]
    m = batch * seq
    alpha = (2.0 * depth) ** 0.25
    k_off = ATTN_WIDTH
    u_off = 3 * ATTN_WIDTH
    gate_off = u_off + 2 * SG_WIDTH
    row = lambda p: p.reshape(1, -1).astype(F32)

    i = jnp.arange(1, N_HEADS + 1, dtype=F32)
    slopes = jnp.exp2(-8.0 * i / N_HEADS)
    col_scale = jnp.where(jnp.arange(w_in.shape[2]) < k_off, LOG2E * QK_DIM ** -0.5, 1.0).astype(F32)

    xf = x.reshape(m, d)
    for l in range(depth):
        lambda_init = 0.8 - 0.6 * math.exp(-0.3 * l)
        w_in_b = (w_in[l] * col_scale).astype(BF16)
        xb = xf.astype(BF16)

        qkv = _proj(xb, w_in_b, 0, 3 * ATTN_WIDTH, gelu=False, out_dtype=BF16)
        sg = _proj(xb, w_in_b, u_off, 2 * SG_WIDTH, gelu=True, out_dtype=F32)

        gd = SG_WIDTH // SG_GROUPS
        bs_full = jnp.repeat(sg_b_s[l].T.astype(F32), gd, axis=1)
        sgo = _sgu(sg, row(sg_ln_g[l]), row(sg_ln_b[l]), sg_w_s[l].astype(BF16), bs_full)

        attn = _attention(qkv.reshape(batch, seq, 3 * ATTN_WIDTH), LOG2E * slopes,
                          row(lambda_q1[l]), row(lambda_k1[l]), row(lambda_q2[l]), row(lambda_k2[l]),
                          row(attn_subln_g[l]), batch=batch, seq=seq, lambda_init=lambda_init)

        merged = _merge(attn.reshape(m, ATTN_WIDTH), sgo, xb, w_in_b, gate_off, row(b_gate[l]),
                        w_branch_attn[l].astype(BF16), w_branch_sg[l].astype(BF16))
        xf = _oproj(merged, w_o[l].astype(BF16), xf, row(ln1_g[l]), row(ln1_b[l]), alpha=alpha)
        xf = _mlp(xf, w_up[l].astype(BF16), w_down[l].astype(BF16), row(ln2_g[l]), row(ln2_b[l]), alpha=alpha)
    return xf.reshape(batch, seq, d)
```

```python
import functools
import math

import jax
import jax.numpy as jnp
from jax import lax
from jax.experimental import pallas as pl
from jax.experimental.pallas import tpu as pltpu

F32 = jnp.float32
BF16 = jnp.bfloat16

N_HEADS = 8
QK_DIM = 64
HEAD_DIM = 2 * QK_DIM
ATTN_WIDTH = N_HEADS * HEAD_DIM
SG_WIDTH = 1024
SG_GROUPS = 8
SG_CHUNK = 128
LN_EPS = 1e-5
INV_SQRT2 = 0.7071067811865476
LOG2E = 1.4426950408889634
LANES = 128

VMEM_LIMIT_BYTES = 60 * 1024 * 1024


def _params(*sem):
    return pltpu.CompilerParams(dimension_semantics=sem, vmem_limit_bytes=VMEM_LIMIT_BYTES)


def _layer_norm(y, g, b):
    mu = jnp.mean(y, axis=-1, keepdims=True)
    d = y - mu
    var = jnp.mean(d * d, axis=-1, keepdims=True)
    return d * lax.rsqrt(var + LN_EPS) * g + b


def _proj_kernel(x_ref, w_ref, o_ref, *, gelu):
    y = jnp.dot(x_ref[...], w_ref[...], preferred_element_type=F32)
    if gelu:
        y = 0.5 * y * (1.0 + lax.erf(y * INV_SQRT2))
    o_ref[...] = y.astype(o_ref.dtype)


def _proj_cast_kernel(x_ref, w_ref, o_ref, xb_ref):
    xb = x_ref[...].astype(BF16)
    xb_ref[...] = xb
    o_ref[...] = jnp.dot(xb, w_ref[...], preferred_element_type=F32).astype(o_ref.dtype)


def _proj_cast(x, w, ncols, *, tm=1024, tn=1024):
    m, k = x.shape
    return pl.pallas_call(
        _proj_cast_kernel,
        grid=(m // tm, ncols // tn),
        in_specs=[pl.BlockSpec((tm, k), lambda i, j: (i, 0)),
                  pl.BlockSpec((k, tn), lambda i, j: (0, j))],
        out_specs=[pl.BlockSpec((tm, tn), lambda i, j: (i, j)),
                   pl.BlockSpec((tm, k), lambda i, j: (i, 0))],
        out_shape=[jax.ShapeDtypeStruct((m, ncols), BF16),
                   jax.ShapeDtypeStruct((m, k), BF16)],
        compiler_params=_params("parallel", "arbitrary"),
    )(x, w)


def _proj(x, w, col0, ncols, *, gelu, out_dtype, tm=1024, tn=1024):
    m, k = x.shape
    j0 = col0 // tn
    return pl.pallas_call(
        functools.partial(_proj_kernel, gelu=gelu),
        grid=(m // tm, ncols // tn),
        in_specs=[pl.BlockSpec((tm, k), lambda i, j: (i, 0)),
                  pl.BlockSpec((k, tn), lambda i, j: (0, j0 + j))],
        out_specs=pl.BlockSpec((tm, tn), lambda i, j: (i, j)),
        out_shape=jax.ShapeDtypeStruct((m, ncols), out_dtype),
        compiler_params=_params("parallel", "arbitrary"),
    )(x, w)


def _sgu_kernel(u_ref, v_ref, g_ref, b_ref, ws_ref, bs_ref, o_ref, vn_ref):
    vn_ref[...] = _layer_norm(v_ref[...], g_ref[...], b_ref[...]).astype(BF16)
    tm = u_ref.shape[0]
    gd = SG_WIDTH // SG_GROUPS
    for c in range(tm // SG_CHUNK):
        rows = slice(c * SG_CHUNK, (c + 1) * SG_CHUNK)
        for g in range(SG_GROUPS):
            cols = slice(g * gd, (g + 1) * gd)
            mixed = jnp.dot(ws_ref[g], vn_ref[rows, cols], preferred_element_type=F32)
            mixed = mixed + bs_ref[:, cols]
            o_ref[rows, cols] = (u_ref[rows, cols] * mixed).astype(o_ref.dtype)


def _sgu(sg, ln_g, ln_b, w_s, bs_full, *, tm=512):
    m = sg.shape[0]
    return pl.pallas_call(
        _sgu_kernel,
        grid=(m // tm,),
        in_specs=[pl.BlockSpec((tm, SG_WIDTH), lambda i: (i, 0)),
                  pl.BlockSpec((tm, SG_WIDTH), lambda i: (i, 1)),
                  pl.BlockSpec((1, SG_WIDTH), lambda i: (0, 0)),
                  pl.BlockSpec((1, SG_WIDTH), lambda i: (0, 0)),
                  pl.BlockSpec(w_s.shape, lambda i: (0, 0, 0)),
                  pl.BlockSpec(bs_full.shape, lambda i: (0, 0))],
        out_specs=pl.BlockSpec((tm, SG_WIDTH), lambda i: (i, 0)),
        out_shape=jax.ShapeDtypeStruct((m, SG_WIDTH), BF16),
        scratch_shapes=[pltpu.VMEM((tm, SG_WIDTH), BF16)],
        compiler_params=_params("parallel"),
    )(sg, sg, ln_g, ln_b, w_s, bs_full)


FEAT_ROWS = 8
ONES_ROWS = 16
NORM_SLACK = 1.02
BOUND_LIMIT = 70.0
HEADROOM = 60.0


def _feature_table(c):
    c1 = c.astype(BF16).astype(F32)
    c2 = (c - c1).astype(BF16).astype(F32)
    c3 = (c - c1 - c2).astype(BF16).astype(F32)
    nh = c.shape[0]
    z = jnp.zeros((nh, LANES), F32)
    ind = lambda lo: jnp.zeros((nh, LANES), F32).at[:, lo:lo + 3].set(1.0)
    pieces = jnp.stack([2 * c1, 2 * c2, 2 * c3, c1, c2, c3], axis=1)
    key_const = z.at[:, 6:12].set(pieces)
    qry_const = z.at[:, 0:6].set(pieces)
    return jnp.stack([ind(0), ind(3), key_const, qry_const, ind(6), ind(9), z, z], axis=1)


def _max_sq_norms(x_ref, groups=8):
    n = x_ref.shape[1] // groups
    g = jnp.square(x_ref[0, 0:n, :].astype(F32))
    for i in range(1, groups):
        g = jnp.maximum(g, jnp.square(x_ref[0, i * n:(i + 1) * n, :].astype(F32)))
    lane = lax.broadcasted_iota(jnp.int32, g.shape, 1)
    out = []
    for mask in (lane < QK_DIM, lane >= QK_DIM):
        n2 = jnp.sum(jnp.where(mask, g, 0.0), axis=1, keepdims=True)
        out.append(jnp.max(n2, axis=0, keepdims=True))
    return out


def _attn_kernel(c_ref, lq1_ref, lk1_ref, lq2_ref, lk2_ref, gain_ref, tab_ref,
                 q_ref, k_ref, v_ref, o_ref,
                 kf_ref, vt_ref, wq_ref, tcorr_ref, mref_ref, m_ref, acc_ref, flag_ref,
                 *, t, lambda_init):
    h = pl.program_id(1)
    qi = pl.program_id(2)
    nk = k_ref.shape[1] // t
    c = c_ref[h]

    @pl.when(qi == 0)
    def _per_head():
        pos = lax.broadcasted_iota(jnp.int32, (t, LANES), 0)
        hi = (pos >> 1).astype(F32)
        lo = (pos & 1).astype(F32)
        tab = tab_ref[0]
        kfeat = (hi * tab[0:1] + lo * tab[1:2] + tab[2:3]).astype(BF16)
        qfeat = tab[3:4] - hi * tab[4:5] - lo * tab[5:6]
        kf_ref[:, 0:LANES] = k_ref[0]
        for j in range(nk):
            kf_ref[j * t:(j + 1) * t, LANES:2 * LANES] = kfeat
            vt_ref[j, 0:HEAD_DIM, :] = v_ref[0, j * t:(j + 1) * t, :].astype(F32).T.astype(BF16)
            vt_ref[j, HEAD_DIM:, :] = jnp.ones((ONES_ROWS, t), BF16)
        for half in range(2):
            rows = slice(half * t, (half + 1) * t)
            wq_ref[0, rows, LANES:2 * LANES] = qfeat.astype(BF16)
            wq_ref[1, rows, LANES:2 * LANES] = (-qfeat).astype(BF16)
        key = lax.broadcasted_iota(jnp.int32, (t, t), 0)
        qry = lax.broadcasted_iota(jnp.int32, (t, t), 1)
        tcorr_ref[...] = (-2.0 * c) * jnp.maximum(key - qry, 0).astype(F32)
        qn = _max_sq_norms(q_ref)
        kn = _max_sq_norms(k_ref)
        bounds = [jnp.sqrt(a * b) * NORM_SLACK for a, b in zip(qn, kn)]
        for half in range(2):
            mref_ref[:, half * t:(half + 1) * t] = jnp.broadcast_to(bounds[half] - HEADROOM, (1, t))
        flag_ref[0] = (jnp.max(jnp.maximum(bounds[0], bounds[1])) <= BOUND_LIMIT).astype(jnp.int32)

    q = q_ref[0, pl.ds(pl.multiple_of(qi * t, t), t), :].astype(F32)
    lane = lax.broadcasted_iota(jnp.int32, q.shape, 1)
    q1 = jnp.where(lane < QK_DIM, q, 0.0).astype(BF16)
    q2 = jnp.where(lane >= QK_DIM, q, 0.0).astype(BF16)
    for sgn in range(2):
        wq_ref[sgn, 0:t, 0:LANES] = q1
        wq_ref[sgn, t:2 * t, 0:LANES] = q2
    fast_ok = flag_ref[0] == 1

    def place(r):
        j = qi + r
        after = j < nk
        return jnp.where(after, j, j - nk), jnp.where(after, 1, 0), jnp.where(after, r, nk - r)

    def block(r):
        jw, sel, dist = (qi, 0, 0) if isinstance(r, int) and r == 0 else place(r)
        kt = kf_ref[pl.ds(pl.multiple_of(jw * t, t), t), :]
        s = lax.dot_general(kt, wq_ref[sel], (((1,), (1,)), ((), ())),
                            preferred_element_type=F32)
        if isinstance(r, int) and r == 0:
            corr = tcorr_ref[...]
            return jnp.concatenate([s[:, 0:t] + corr, s[:, t:2 * t] + corr], axis=1), 0.0, vt_ref[jw]
        return s, -c * (dist * t).astype(F32), vt_ref[jw]


    @pl.when(fast_ok)
    def _fixed_reference():
        m0 = mref_ref[...]
        for r in range(nk):
            s, shift, vt = block(r)
            pv = jnp.dot(vt, jnp.exp2(s - (m0 - shift)).astype(BF16), preferred_element_type=F32)
            if r == 0:
                acc_ref[...] = pv
            else:
                acc_ref[...] += pv

    @pl.when(jnp.logical_not(fast_ok))
    def _running_max():
        s, _, vt = block(0)
        m_ref[...] = jnp.max(s, axis=0, keepdims=True)
        acc_ref[...] = jnp.dot(vt, jnp.exp2(s - m_ref[...]).astype(BF16), preferred_element_type=F32)

        def fold(r, carry):
            s, shift, vt = block(r)
            m_old = m_ref[...]
            m_new = jnp.maximum(m_old, jnp.max(s, axis=0, keepdims=True) + shift)
            pv = jnp.dot(vt, jnp.exp2(s - (m_new - shift)).astype(BF16), preferred_element_type=F32)
            acc_ref[...] = jnp.exp2(m_old - m_new) * acc_ref[...] + pv
            m_ref[...] = m_new
            return carry

        lax.fori_loop(1, nk, fold, 0)

    lam = (jnp.exp(jnp.sum(lq1_ref[...] * lk1_ref[...], axis=-1, keepdims=True))
           - jnp.exp(jnp.sum(lq2_ref[...] * lk2_ref[...], axis=-1, keepdims=True))
           + lambda_init)
    rl = 1.0 / acc_ref[HEAD_DIM:HEAD_DIM + 1, :]
    o = (acc_ref[0:HEAD_DIM, 0:t] * rl[:, 0:t]
         - acc_ref[0:HEAD_DIM, t:2 * t] * (lam * rl[:, t:2 * t]))
    ms = jnp.mean(o * o, axis=0, keepdims=True)
    o = o * lax.rsqrt(ms + LN_EPS)
    o = o.T * gain_ref[...] * (1.0 - lambda_init)
    o_ref[0] = o.astype(o_ref.dtype)


def _attention(qkv, c, lq1, lk1, lq2, lk2, gain, *, batch, seq, lambda_init, t=512):
    nh = N_HEADS
    vec = lambda n: pl.BlockSpec((1, n), lambda b, h, i: (0, 0))
    return pl.pallas_call(
        functools.partial(_attn_kernel, t=t, lambda_init=lambda_init),
        grid=(batch, nh, seq // t),
        in_specs=[pl.BlockSpec(memory_space=pltpu.SMEM),
                  vec(QK_DIM), vec(QK_DIM), vec(QK_DIM), vec(QK_DIM), vec(HEAD_DIM),
                  pl.BlockSpec((1, FEAT_ROWS, LANES), lambda b, h, i: (h, 0, 0)),
                  pl.BlockSpec((1, seq, HEAD_DIM), lambda b, h, i: (b, 0, h)),
                  pl.BlockSpec((1, seq, HEAD_DIM), lambda b, h, i: (b, 0, nh + h)),
                  pl.BlockSpec((1, seq, HEAD_DIM), lambda b, h, i: (b, 0, 2 * nh + h))],
        out_specs=pl.BlockSpec((1, t, HEAD_DIM), lambda b, h, i: (b, i, h)),
        out_shape=jax.ShapeDtypeStruct((batch, seq, ATTN_WIDTH), BF16),
        scratch_shapes=[pltpu.VMEM((seq, 2 * LANES), BF16),
                        pltpu.VMEM((seq // t, HEAD_DIM + ONES_ROWS, t), BF16),
                        pltpu.VMEM((2, 2 * t, 2 * LANES), BF16),
                        pltpu.VMEM((t, t), F32),
                        pltpu.VMEM((1, 2 * t), F32),
                        pltpu.VMEM((1, 2 * t), F32),
                        pltpu.VMEM((HEAD_DIM + ONES_ROWS, 2 * t), F32),
                        pltpu.SMEM((1,), jnp.int32)],
        compiler_params=_params("parallel", "arbitrary", "arbitrary"),
    )(c, lq1, lk1, lq2, lk2, gain, _feature_table(c), qkv, qkv, qkv)


def _merge_kernel(a_ref, s_ref, x_ref, wa_ref, ws_ref, wg0_ref, wg1_ref, bg0_ref, bg1_ref, o_ref):
    x = x_ref[...]
    g0 = jax.nn.sigmoid(jnp.dot(x, wg0_ref[...], preferred_element_type=F32) + bg0_ref[...])
    g1 = jax.nn.sigmoid(jnp.dot(x, wg1_ref[...], preferred_element_type=F32) + bg1_ref[...])
    ya = jnp.dot(a_ref[...], wa_ref[...], preferred_element_type=F32)
    ys = jnp.dot(s_ref[...], ws_ref[...], preferred_element_type=F32)
    o_ref[...] = (g0 * ya + g1 * ys).astype(o_ref.dtype)


def _merge(attn, sgo, x, w_in, gate_off, b_gate, w_ba, w_bs, *, tm=1024, tn=512):
    m, d = x.shape
    j0 = gate_off // tn
    j1 = (gate_off + d) // tn
    nb = d // tn
    return pl.pallas_call(
        _merge_kernel,
        grid=(m // tm, d // tn),
        in_specs=[pl.BlockSpec((tm, attn.shape[1]), lambda i, j: (i, 0)),
                  pl.BlockSpec((tm, sgo.shape[1]), lambda i, j: (i, 0)),
                  pl.BlockSpec((tm, d), lambda i, j: (i, 0)),
                  pl.BlockSpec((w_ba.shape[0], tn), lambda i, j: (0, j)),
                  pl.BlockSpec((w_bs.shape[0], tn), lambda i, j: (0, j)),
                  pl.BlockSpec((d, tn), lambda i, j: (0, j0 + j)),
                  pl.BlockSpec((d, tn), lambda i, j: (0, j1 + j)),
                  pl.BlockSpec((1, tn), lambda i, j: (0, j)),
                  pl.BlockSpec((1, tn), lambda i, j: (0, nb + j))],
        out_specs=pl.BlockSpec((tm, tn), lambda i, j: (i, j)),
        out_shape=jax.ShapeDtypeStruct((m, d), BF16),
        compiler_params=_params("parallel", "arbitrary"),
    )(attn, sgo, x, w_ba, w_bs, w_in, w_in, b_gate, b_gate)


def _oproj_kernel(y_ref, w_ref, x_ref, g_ref, b_ref, o_ref, *, alpha):
    y = jnp.dot(y_ref[...], w_ref[...], preferred_element_type=F32)
    o_ref[...] = _layer_norm(alpha * x_ref[...] + y, g_ref[...], b_ref[...])


def _oproj(merged, w_o, x, g, b, *, alpha, tm=512):
    m, d = x.shape
    return pl.pallas_call(
        functools.partial(_oproj_kernel, alpha=alpha),
        grid=(m // tm,),
        in_specs=[pl.BlockSpec((tm, d), lambda i: (i, 0)),
                  pl.BlockSpec((d, d), lambda i: (0, 0)),
                  pl.BlockSpec((tm, d), lambda i: (i, 0)),
                  pl.BlockSpec((1, d), lambda i: (0, 0)),
                  pl.BlockSpec((1, d), lambda i: (0, 0))],
        out_specs=pl.BlockSpec((tm, d), lambda i: (i, 0)),
        out_shape=jax.ShapeDtypeStruct((m, d), F32),
        compiler_params=_params("parallel"),
    )(merged, w_o, x, g, b)


def _mlp_kernel(x_ref, wu_ref, wd_ref, g_ref, b_ref, o_ref, xb_ref, *, alpha):
    f = pl.program_id(1)

    @pl.when(f == 0)
    def _():
        xb_ref[...] = x_ref[...].astype(BF16)
        o_ref[...] = jnp.zeros_like(o_ref)

    hid = jnp.dot(xb_ref[...], wu_ref[...], preferred_element_type=F32)
    hid = jnp.square(jnp.maximum(hid, 0.0)).astype(BF16)
    o_ref[...] += jnp.dot(hid, wd_ref[...], preferred_element_type=F32)

    @pl.when(f == pl.num_programs(1) - 1)
    def _():
        o_ref[...] = _layer_norm(alpha * x_ref[...] + o_ref[...], g_ref[...], b_ref[...])


def _mlp(x, w_up, w_down, g, b, *, alpha, tm=512, tf=1024):
    m, d = x.shape
    dff = w_up.shape[1]
    return pl.pallas_call(
        functools.partial(_mlp_kernel, alpha=alpha),
        grid=(m // tm, dff // tf),
        in_specs=[pl.BlockSpec((tm, d), lambda i, f: (i, 0)),
                  pl.BlockSpec((d, tf), lambda i, f: (0, f)),
                  pl.BlockSpec((tf, d), lambda i, f: (f, 0)),
                  pl.BlockSpec((1, d), lambda i, f: (0, 0)),
                  pl.BlockSpec((1, d), lambda i, f: (0, 0))],
        out_specs=pl.BlockSpec((tm, d), lambda i, f: (i, 0)),
        out_shape=jax.ShapeDtypeStruct((m, d), F32),
        scratch_shapes=[pltpu.VMEM((tm, d), BF16)],
        compiler_params=_params("parallel", "arbitrary"),
    )(x, w_up, w_down, g, b)


def kernel(x, w_in, lambda_q1, lambda_k1, lambda_q2, lambda_k2, attn_subln_g, sg_ln_g, sg_ln_b, sg_w_s, sg_b_s, b_gate, w_branch_attn, w_branch_sg, w_o, ln1_g, ln1_b, w_up, w_down, ln2_g, ln2_b):
    batch, seq, d = x.shape
    depth = w_in.shape[0]
    m = batch * seq
    alpha = (2.0 * depth) ** 0.25
    k_off = ATTN_WIDTH
    u_off = 3 * ATTN_WIDTH
    gate_off = u_off + 2 * SG_WIDTH
    row = lambda p: p.reshape(1, -1).astype(F32)

    i = jnp.arange(1, N_HEADS + 1, dtype=F32)
    slopes = jnp.exp2(-8.0 * i / N_HEADS)
    col_scale = jnp.where(jnp.arange(w_in.shape[2]) < k_off, LOG2E * QK_DIM ** -0.5, 1.0).astype(F32)

    xf = x.reshape(m, d)
    for l in range(depth):
        lambda_init = 0.8 - 0.6 * math.exp(-0.3 * l)
        w_in_b = (w_in[l] * col_scale).astype(BF16)
        qkv, xb = _proj_cast(xf, w_in_b, 3 * ATTN_WIDTH)
        sg = _proj(xb, w_in_b, u_off, 2 * SG_WIDTH, gelu=True, out_dtype=F32)

        gd = SG_WIDTH // SG_GROUPS
        bs_full = jnp.repeat(sg_b_s[l].T.astype(F32), gd, axis=1)
        sgo = _sgu(sg, row(sg_ln_g[l]), row(sg_ln_b[l]), sg_w_s[l].astype(BF16), bs_full)

        attn = _attention(qkv.reshape(batch, seq, 3 * ATTN_WIDTH), LOG2E * slopes,
                          row(lambda_q1[l]), row(lambda_k1[l]), row(lambda_q2[l]), row(lambda_k2[l]),
                          row(attn_subln_g[l]), batch=batch, seq=seq, lambda_init=lambda_init)

        merged = _merge(attn.reshape(m, ATTN_WIDTH), sgo, xb, w_in_b, gate_off, row(b_gate[l]),
                        w_branch_attn[l].astype(BF16), w_branch_sg[l].astype(BF16))
        xf = _oproj(merged, w_o[l].astype(BF16), xf, row(ln1_g[l]), row(ln1_b[l]), alpha=alpha)
        xf = _mlp(xf, w_up[l].astype(BF16), w_down[l].astype(BF16), row(ln2_g[l]), row(ln2_b[l]), alpha=alpha)
    return xf.reshape(batch, seq, d)
```

```python
import functools
import math

import jax
import jax.numpy as jnp
from jax import lax
from jax.experimental import pallas as pl
from jax.experimental.pallas import tpu as pltpu

F32 = jnp.float32
BF16 = jnp.bfloat16

N_HEADS = 8
QK_DIM = 64
HEAD_DIM = 2 * QK_DIM
ATTN_WIDTH = N_HEADS * HEAD_DIM
SG_WIDTH = 1024
SG_GROUPS = 8
SG_CHUNK = 128
LN_EPS = 1e-5
INV_SQRT2 = 0.7071067811865476
LOG2E = 1.4426950408889634
LANES = 128

VMEM_LIMIT_BYTES = 60 * 1024 * 1024


def _params(*sem):
    return pltpu.CompilerParams(dimension_semantics=sem, vmem_limit_bytes=VMEM_LIMIT_BYTES)


def _layer_norm(y, g, b):
    mu = jnp.mean(y, axis=-1, keepdims=True)
    d = y - mu
    var = jnp.mean(d * d, axis=-1, keepdims=True)
    return d * lax.rsqrt(var + LN_EPS) * g + b


def _proj_kernel(x_ref, w_ref, o_ref, *, gelu):
    y = jnp.dot(x_ref[...], w_ref[...], preferred_element_type=F32)
    if gelu:
        y = 0.5 * y * (1.0 + lax.erf(y * INV_SQRT2))
    o_ref[...] = y.astype(o_ref.dtype)


def _proj_cast_kernel(x_ref, w_ref, o_ref, xb_ref):
    xb = x_ref[...].astype(BF16)
    xb_ref[...] = xb
    o_ref[...] = jnp.dot(xb, w_ref[...], preferred_element_type=F32).astype(o_ref.dtype)


def _resident(shape):
    return pl.BlockSpec(shape, lambda i: (0,) * len(shape), pipeline_mode=pl.Buffered(1))


def _proj_cast(x, w, *, tm=512):
    m, k = x.shape
    n = w.shape[1]
    return pl.pallas_call(
        _proj_cast_kernel,
        grid=(m // tm,),
        in_specs=[pl.BlockSpec((tm, k), lambda i: (i, 0)), _resident((k, n))],
        out_specs=[pl.BlockSpec((tm, n), lambda i: (i, 0)),
                   pl.BlockSpec((tm, k), lambda i: (i, 0))],
        out_shape=[jax.ShapeDtypeStruct((m, n), BF16),
                   jax.ShapeDtypeStruct((m, k), BF16)],
        compiler_params=_params("parallel"),
    )(x, w)


def _proj(x, w, *, gelu, out_dtype, tm=512):
    m, k = x.shape
    n = w.shape[1]
    return pl.pallas_call(
        functools.partial(_proj_kernel, gelu=gelu),
        grid=(m // tm,),
        in_specs=[pl.BlockSpec((tm, k), lambda i: (i, 0)), _resident((k, n))],
        out_specs=pl.BlockSpec((tm, n), lambda i: (i, 0)),
        out_shape=jax.ShapeDtypeStruct((m, n), out_dtype),
        compiler_params=_params("parallel"),
    )(x, w)


def _sgu_kernel(u_ref, v_ref, g_ref, b_ref, ws_ref, bs_ref, o_ref, vn_ref):
    vn_ref[...] = _layer_norm(v_ref[...], g_ref[...], b_ref[...]).astype(BF16)
    tm = u_ref.shape[0]
    gd = SG_WIDTH // SG_GROUPS
    for c in range(tm // SG_CHUNK):
        rows = slice(c * SG_CHUNK, (c + 1) * SG_CHUNK)
        for g in range(SG_GROUPS):
            cols = slice(g * gd, (g + 1) * gd)
            mixed = jnp.dot(ws_ref[g], vn_ref[rows, cols], preferred_element_type=F32)
            mixed = mixed + bs_ref[:, cols]
            o_ref[rows, cols] = (u_ref[rows, cols] * mixed).astype(o_ref.dtype)


def _sgu(sg, ln_g, ln_b, w_s, bs_full, *, tm=512):
    m = sg.shape[0]
    return pl.pallas_call(
        _sgu_kernel,
        grid=(m // tm,),
        in_specs=[pl.BlockSpec((tm, SG_WIDTH), lambda i: (i, 0)),
                  pl.BlockSpec((tm, SG_WIDTH), lambda i: (i, 1)),
                  pl.BlockSpec((1, SG_WIDTH), lambda i: (0, 0)),
                  pl.BlockSpec((1, SG_WIDTH), lambda i: (0, 0)),
                  pl.BlockSpec(w_s.shape, lambda i: (0, 0, 0)),
                  pl.BlockSpec(bs_full.shape, lambda i: (0, 0))],
        out_specs=pl.BlockSpec((tm, SG_WIDTH), lambda i: (i, 0)),
        out_shape=jax.ShapeDtypeStruct((m, SG_WIDTH), BF16),
        scratch_shapes=[pltpu.VMEM((tm, SG_WIDTH), BF16)],
        compiler_params=_params("parallel"),
    )(sg, sg, ln_g, ln_b, w_s, bs_full)


FEAT_ROWS = 16
ONES_ROWS = 16
BEFORE, AFTER, DIAG = 0, 1, 2
NORM_SLACK = 1.02
BOUND_LIMIT = 70.0
HEADROOM = 60.0


def _pos_lo_bits(t):
    return max(0, t.bit_length() - 9)


def _feature_table(c, t):
    radix = float(1 << _pos_lo_bits(t))
    c1 = c.astype(BF16).astype(F32)
    c2 = (c - c1).astype(BF16).astype(F32)
    c3 = (c - c1 - c2).astype(BF16).astype(F32)
    nh = c.shape[0]
    z = jnp.zeros((nh, LANES), F32)
    ind = lambda lo: jnp.zeros((nh, LANES), F32).at[:, lo:lo + 3].set(1.0)
    pieces = jnp.stack([radix * c1, radix * c2, radix * c3, c1, c2, c3], axis=1)
    at = lambda lo: z.at[:, lo:lo + 6].set(pieces)
    rows = [ind(0), ind(3), ind(6), ind(9), at(12),
            at(6), at(0), ind(12), ind(15)]
    return jnp.stack(rows + [z] * (FEAT_ROWS - len(rows)), axis=1)


def _max_sq_norms(x_ref, groups=8):
    n = x_ref.shape[1] // groups
    g = jnp.square(x_ref[0, 0:n, :].astype(F32))
    for i in range(1, groups):
        g = jnp.maximum(g, jnp.square(x_ref[0, i * n:(i + 1) * n, :].astype(F32)))
    lane = lax.broadcasted_iota(jnp.int32, g.shape, 1)
    out = []
    for mask in (lane < QK_DIM, lane >= QK_DIM):
        n2 = jnp.sum(jnp.where(mask, g, 0.0), axis=1, keepdims=True)
        out.append(jnp.max(n2, axis=0, keepdims=True))
    return out


def _attn_kernel(c_ref, lq1_ref, lk1_ref, lq2_ref, lk2_ref, gain_ref, tab_ref,
                 q_ref, k_ref, v_ref, o_ref,
                 kf_ref, vt_ref, wq_ref, tbase_ref, tdiag_ref, mref_ref, m_ref, acc_ref, flag_ref,
                 *, t, lambda_init):
    h = pl.program_id(1)
    qi = pl.program_id(2)
    nk = k_ref.shape[1] // t
    c = c_ref[h]

    @pl.when((pl.program_id(0) == 0) & (h == 0) & (qi == 0))
    def _once():
        key = lax.broadcasted_iota(jnp.int32, (t, t), 0)
        qry = lax.broadcasted_iota(jnp.int32, (t, t), 1)
        tbase_ref[...] = jnp.abs(key - qry).astype(F32)

    @pl.when(qi == 0)
    def _per_head():
        tdiag_ref[...] = -c * tbase_ref[...]
        pos = lax.broadcasted_iota(jnp.int32, (t, LANES), 0)
        lo_bits = _pos_lo_bits(t)
        split = lambda x: ((x >> lo_bits).astype(F32), (x & ((1 << lo_bits) - 1)).astype(F32))
        hi, lo = split(pos)
        rhi, rlo = split(t - pos)
        tab = tab_ref[0]
        kfeat = (hi * tab[0:1] + lo * tab[1:2] + rhi * tab[2:3] + rlo * tab[3:4] + tab[4:5]).astype(BF16)
        q_before = -(tab[5:6] + hi * tab[7:8] + lo * tab[8:9])
        q_after = -(tab[6:7] + rhi * tab[7:8] + rlo * tab[8:9])
        kf_ref[:, 0:LANES] = k_ref[0]
        for j in range(nk):
            kf_ref[j * t:(j + 1) * t, LANES:2 * LANES] = kfeat
            vt_ref[j, 0:HEAD_DIM, :] = v_ref[0, j * t:(j + 1) * t, :].astype(F32).T.astype(BF16)
            vt_ref[j, HEAD_DIM:, :] = jnp.ones((ONES_ROWS, t), BF16)
        for half in range(2):
            rows = slice(half * t, (half + 1) * t)
            wq_ref[BEFORE, rows, LANES:2 * LANES] = q_before.astype(BF16)
            wq_ref[AFTER, rows, LANES:2 * LANES] = q_after.astype(BF16)
            wq_ref[DIAG, rows, LANES:2 * LANES] = jnp.zeros((t, LANES), BF16)
        qn = _max_sq_norms(q_ref)
        kn = _max_sq_norms(k_ref)
        bounds = [jnp.sqrt(a * b) * NORM_SLACK for a, b in zip(qn, kn)]
        for half in range(2):
            mref_ref[:, half * t:(half + 1) * t] = jnp.broadcast_to(bounds[half] - HEADROOM, (1, t))
        flag_ref[0] = (jnp.max(jnp.maximum(bounds[0], bounds[1])) <= BOUND_LIMIT).astype(jnp.int32)

    q = q_ref[0, pl.ds(pl.multiple_of(qi * t, t), t), :].astype(F32)
    lane = lax.broadcasted_iota(jnp.int32, q.shape, 1)
    q1 = jnp.where(lane < QK_DIM, q, 0.0).astype(BF16)
    q2 = jnp.where(lane >= QK_DIM, q, 0.0).astype(BF16)
    for variant in (BEFORE, AFTER, DIAG):
        wq_ref[variant, 0:t, 0:LANES] = q1
        wq_ref[variant, t:2 * t, 0:LANES] = q2
    fast_ok = flag_ref[0] == 1

    def place(r):
        j = qi + r
        after = j < nk
        return (jnp.where(after, j, j - nk), jnp.where(after, AFTER, BEFORE),
                jnp.where(after, r, nk - r))

    def block(r):
        jw, variant, dist = (qi, DIAG, 0) if isinstance(r, int) and r == 0 else place(r)
        kt = kf_ref[pl.ds(pl.multiple_of(jw * t, t), t), :]
        s = lax.dot_general(kt, wq_ref[variant], (((1,), (1,)), ((), ())),
                            preferred_element_type=F32)
        if isinstance(r, int) and r == 0:
            bias = tdiag_ref[...]
            return jnp.concatenate([s[:, 0:t] + bias, s[:, t:2 * t] + bias], axis=1), 0.0, vt_ref[jw]
        return s, -c * ((dist - 1) * t).astype(F32), vt_ref[jw]


    @pl.when(fast_ok)
    def _fixed_reference():
        m0 = mref_ref[...]
        for r in range(nk):
            s, shift, vt = block(r)
            pv = jnp.dot(vt, jnp.exp2(s - (m0 - shift)).astype(BF16), preferred_element_type=F32)
            if r == 0:
                acc_ref[...] = pv
            else:
                acc_ref[...] += pv

    @pl.when(jnp.logical_not(fast_ok))
    def _running_max():
        s, _, vt = block(0)
        m_ref[...] = jnp.max(s, axis=0, keepdims=True)
        acc_ref[...] = jnp.dot(vt, jnp.exp2(s - m_ref[...]).astype(BF16), preferred_element_type=F32)

        def fold(r, carry):
            s, shift, vt = block(r)
            m_old = m_ref[...]
            m_new = jnp.maximum(m_old, jnp.max(s, axis=0, keepdims=True) + shift)
            pv = jnp.dot(vt, jnp.exp2(s - (m_new - shift)).astype(BF16), preferred_element_type=F32)
            acc_ref[...] = jnp.exp2(m_old - m_new) * acc_ref[...] + pv
            m_ref[...] = m_new
            return carry

        lax.fori_loop(1, nk, fold, 0)

    lam = (jnp.exp(jnp.sum(lq1_ref[...] * lk1_ref[...], axis=-1, keepdims=True))
           - jnp.exp(jnp.sum(lq2_ref[...] * lk2_ref[...], axis=-1, keepdims=True))
           + lambda_init)
    rl = 1.0 / acc_ref[HEAD_DIM:HEAD_DIM + 1, :]
    o = (acc_ref[0:HEAD_DIM, 0:t] * rl[:, 0:t]
         - acc_ref[0:HEAD_DIM, t:2 * t] * (lam * rl[:, t:2 * t]))
    ms = jnp.mean(o * o, axis=0, keepdims=True)
    o = o * lax.rsqrt(ms + LN_EPS)
    o = o.T * gain_ref[...] * (1.0 - lambda_init)
    o_ref[0] = o.astype(o_ref.dtype)


def _attention(qkv, c, lq1, lk1, lq2, lk2, gain, *, batch, seq, lambda_init, t=1024):
    nh = N_HEADS
    vec = lambda n: pl.BlockSpec((1, n), lambda b, h, i: (0, 0))
    return pl.pallas_call(
        functools.partial(_attn_kernel, t=t, lambda_init=lambda_init),
        grid=(batch, nh, seq // t),
        in_specs=[pl.BlockSpec(memory_space=pltpu.SMEM),
                  vec(QK_DIM), vec(QK_DIM), vec(QK_DIM), vec(QK_DIM), vec(HEAD_DIM),
                  pl.BlockSpec((1, FEAT_ROWS, LANES), lambda b, h, i: (h, 0, 0)),
                  pl.BlockSpec((1, seq, HEAD_DIM), lambda b, h, i: (b, 0, h)),
                  pl.BlockSpec((1, seq, HEAD_DIM), lambda b, h, i: (b, 0, nh + h)),
                  pl.BlockSpec((1, seq, HEAD_DIM), lambda b, h, i: (b, 0, 2 * nh + h))],
        out_specs=pl.BlockSpec((1, t, HEAD_DIM), lambda b, h, i: (b, i, h)),
        out_shape=jax.ShapeDtypeStruct((batch, seq, ATTN_WIDTH), BF16),
        scratch_shapes=[pltpu.VMEM((seq, 2 * LANES), BF16),
                        pltpu.VMEM((seq // t, HEAD_DIM + ONES_ROWS, t), BF16),
                        pltpu.VMEM((3, 2 * t, 2 * LANES), BF16),
                        pltpu.VMEM((t, t), F32),
                        pltpu.VMEM((t, t), F32),
                        pltpu.VMEM((1, 2 * t), F32),
                        pltpu.VMEM((1, 2 * t), F32),
                        pltpu.VMEM((HEAD_DIM + ONES_ROWS, 2 * t), F32),
                        pltpu.SMEM((1,), jnp.int32)],
        compiler_params=_params("arbitrary", "arbitrary", "arbitrary"),
    )(c, lq1, lk1, lq2, lk2, gain, _feature_table(c, t), qkv, qkv, qkv)


def _merge_kernel(a_ref, s_ref, x_ref, wa_ref, ws_ref, wg0_ref, wg1_ref, bg0_ref, bg1_ref, o_ref):
    x = x_ref[...]
    g0 = jax.nn.sigmoid(jnp.dot(x, wg0_ref[...], preferred_element_type=F32) + bg0_ref[...])
    g1 = jax.nn.sigmoid(jnp.dot(x, wg1_ref[...], preferred_element_type=F32) + bg1_ref[...])
    ya = jnp.dot(a_ref[...], wa_ref[...], preferred_element_type=F32)
    ys = jnp.dot(s_ref[...], ws_ref[...], preferred_element_type=F32)
    o_ref[...] = (g0 * ya + g1 * ys).astype(o_ref.dtype)


def _merge(attn, sgo, x, w_gate, b_gate, w_ba, w_bs, *, tm=1024, tn=512):
    m, d = x.shape
    nb = d // tn
    return pl.pallas_call(
        _merge_kernel,
        grid=(m // tm, d // tn),
        in_specs=[pl.BlockSpec((tm, attn.shape[1]), lambda i, j: (i, 0)),
                  pl.BlockSpec((tm, sgo.shape[1]), lambda i, j: (i, 0)),
                  pl.BlockSpec((tm, d), lambda i, j: (i, 0)),
                  pl.BlockSpec((w_ba.shape[0], tn), lambda i, j: (0, j)),
                  pl.BlockSpec((w_bs.shape[0], tn), lambda i, j: (0, j)),
                  pl.BlockSpec((d, tn), lambda i, j: (0, j)),
                  pl.BlockSpec((d, tn), lambda i, j: (0, nb + j)),
                  pl.BlockSpec((1, tn), lambda i, j: (0, j)),
                  pl.BlockSpec((1, tn), lambda i, j: (0, nb + j))],
        out_specs=pl.BlockSpec((tm, tn), lambda i, j: (i, j)),
        out_shape=jax.ShapeDtypeStruct((m, d), BF16),
        compiler_params=_params("parallel", "arbitrary"),
    )(attn, sgo, x, w_ba, w_bs, w_gate, w_gate, b_gate, b_gate)


def _oproj_kernel(y_ref, w_ref, x_ref, g_ref, b_ref, o_ref, *, alpha):
    for r in range(y_ref.shape[0] // OPROJ_PIECE_ROWS):
        rows = slice(r * OPROJ_PIECE_ROWS, (r + 1) * OPROJ_PIECE_ROWS)
        y = jnp.dot(y_ref[rows, :], w_ref[...], preferred_element_type=F32)
        o_ref[rows, :] = _layer_norm(alpha * x_ref[rows, :] + y, g_ref[...], b_ref[...])


OPROJ_PIECE_ROWS = 256


def _oproj(merged, w_o, x, g, b, *, alpha, tm=1024):
    m, d = x.shape
    return pl.pallas_call(
        functools.partial(_oproj_kernel, alpha=alpha),
        grid=(m // tm,),
        in_specs=[pl.BlockSpec((tm, d), lambda i: (i, 0)),
                  _resident((d, d)),
                  pl.BlockSpec((tm, d), lambda i: (i, 0)),
                  pl.BlockSpec((1, d), lambda i: (0, 0)),
                  pl.BlockSpec((1, d), lambda i: (0, 0))],
        out_specs=pl.BlockSpec((tm, d), lambda i: (i, 0)),
        out_shape=jax.ShapeDtypeStruct((m, d), F32),
        compiler_params=_params("parallel"),
    )(merged, w_o, x, g, b)


def _mlp_kernel(x_ref, wu_ref, wd_ref, g_ref, b_ref, o_ref, xb_ref, *, alpha):
    f = pl.program_id(1)

    @pl.when(f == 0)
    def _():
        xb_ref[...] = x_ref[...].astype(BF16)
        o_ref[...] = jnp.zeros_like(o_ref)

    hid = jnp.dot(xb_ref[...], wu_ref[...], preferred_element_type=F32)
    hid = jnp.square(jnp.maximum(hid, 0.0)).astype(BF16)
    o_ref[...] += jnp.dot(hid, wd_ref[...], preferred_element_type=F32)

    @pl.when(f == pl.num_programs(1) - 1)
    def _():
        o_ref[...] = _layer_norm(alpha * x_ref[...] + o_ref[...], g_ref[...], b_ref[...])


def _mlp(x, w_up, w_down, g, b, *, alpha, tm=512, tf=1024):
    m, d = x.shape
    dff = w_up.shape[1]
    return pl.pallas_call(
        functools.partial(_mlp_kernel, alpha=alpha),
        grid=(m // tm, dff // tf),
        in_specs=[pl.BlockSpec((tm, d), lambda i, f: (i, 0)),
                  pl.BlockSpec((d, tf), lambda i, f: (0, f)),
                  pl.BlockSpec((tf, d), lambda i, f: (f, 0)),
                  pl.BlockSpec((1, d), lambda i, f: (0, 0)),
                  pl.BlockSpec((1, d), lambda i, f: (0, 0))],
        out_specs=pl.BlockSpec((tm, d), lambda i, f: (i, 0)),
        out_shape=jax.ShapeDtypeStruct((m, d), F32),
        scratch_shapes=[pltpu.VMEM((tm, d), BF16)],
        compiler_params=_params("parallel", "arbitrary"),
    )(x, w_up, w_down, g, b)


def kernel(x, w_in, lambda_q1, lambda_k1, lambda_q2, lambda_k2, attn_subln_g, sg_ln_g, sg_ln_b, sg_w_s, sg_b_s, b_gate, w_branch_attn, w_branch_sg, w_o, ln1_g, ln1_b, w_up, w_down, ln2_g, ln2_b):
    batch, seq, d = x.shape
    depth = w_in.shape[0]
    m = batch * seq
    alpha = (2.0 * depth) ** 0.25
    k_off = ATTN_WIDTH
    u_off = 3 * ATTN_WIDTH
    gate_off = u_off + 2 * SG_WIDTH
    row = lambda p: p.reshape(1, -1).astype(F32)

    i = jnp.arange(1, N_HEADS + 1, dtype=F32)
    slopes = jnp.exp2(-8.0 * i / N_HEADS)
    col_scale = jnp.where(jnp.arange(w_in.shape[2]) < k_off, LOG2E * QK_DIM ** -0.5, 1.0).astype(F32)

    xf = x.reshape(m, d)
    for l in range(depth):
        lambda_init = 0.8 - 0.6 * math.exp(-0.3 * l)
        w_in_b = (w_in[l] * col_scale).astype(BF16)
        qkv, xb = _proj_cast(xf, w_in_b[:, :u_off])
        sg = _proj(xb, w_in_b[:, u_off:gate_off], gelu=True, out_dtype=F32)

        gd = SG_WIDTH // SG_GROUPS
        bs_full = jnp.repeat(sg_b_s[l].T.astype(F32), gd, axis=1)
        sgo = _sgu(sg, row(sg_ln_g[l]), row(sg_ln_b[l]), sg_w_s[l].astype(BF16), bs_full)

        attn = _attention(qkv.reshape(batch, seq, 3 * ATTN_WIDTH), LOG2E * slopes,
                          row(lambda_q1[l]), row(lambda_k1[l]), row(lambda_q2[l]), row(lambda_k2[l]),
                          row(attn_subln_g[l]), batch=batch, seq=seq, lambda_init=lambda_init)

        merged = _merge(attn.reshape(m, ATTN_WIDTH), sgo, xb, w_in_b[:, gate_off:], row(b_gate[l]),
                        w_branch_attn[l].astype(BF16), w_branch_sg[l].astype(BF16))
        xf = _oproj(merged, w_o[l].astype(BF16), xf, row(ln1_g[l]), row(ln1_b[l]), alpha=alpha)
        xf = _mlp(xf, w_up[l].astype(BF16), w_down[l].astype(BF16), row(ln2_g[l]), row(ln2_b[l]), alpha=alpha)
    return xf.reshape(batch, seq, d)
```

```python
import functools
import math

import jax
import jax.numpy as jnp
from jax import lax
from jax.experimental import pallas as pl
from jax.experimental.pallas import tpu as pltpu

F32 = jnp.float32
BF16 = jnp.bfloat16

N_HEADS = 8
QK_DIM = 64
HEAD_DIM = 2 * QK_DIM
ATTN_WIDTH = N_HEADS * HEAD_DIM
SG_WIDTH = 1024
SG_GROUPS = 8
SG_CHUNK = 128
LN_EPS = 1e-5
INV_SQRT2 = 0.7071067811865476
LOG2E = 1.4426950408889634
LANES = 128

VMEM_LIMIT_BYTES = 60 * 1024 * 1024


def _params(*sem):
    return pltpu.CompilerParams(dimension_semantics=sem, vmem_limit_bytes=VMEM_LIMIT_BYTES)


def _layer_norm(y, g, b):
    mu = jnp.mean(y, axis=-1, keepdims=True)
    d = y - mu
    var = jnp.mean(d * d, axis=-1, keepdims=True)
    return d * lax.rsqrt(var + LN_EPS) * g + b


def _proj_cast_kernel(x_ref, w_ref, o_ref, xb_ref):
    xb = x_ref[...].astype(BF16)
    xb_ref[...] = xb
    o_ref[...] = jnp.dot(xb, w_ref[...], preferred_element_type=F32).astype(o_ref.dtype)


def _resident(shape):
    return pl.BlockSpec(shape, lambda i: (0,) * len(shape), pipeline_mode=pl.Buffered(1))


def _proj_cast(x, w, *, tm=512):
    m, k = x.shape
    n = w.shape[1]
    return pl.pallas_call(
        _proj_cast_kernel,
        grid=(m // tm,),
        in_specs=[pl.BlockSpec((tm, k), lambda i: (i, 0)), _resident((k, n))],
        out_specs=[pl.BlockSpec((tm, n), lambda i: (i, 0)),
                   pl.BlockSpec((tm, k), lambda i: (i, 0))],
        out_shape=[jax.ShapeDtypeStruct((m, n), BF16),
                   jax.ShapeDtypeStruct((m, k), BF16)],
        compiler_params=_params("parallel"),
    )(x, w)


SG_PIECE_ROWS = 256


def _sg_branch_kernel(x_ref, w_ref, g_ref, b_ref, ws_ref, bs_ref, o_ref, vn_ref):
    chunks = SG_PIECE_ROWS // SG_CHUNK
    gd = SG_WIDTH // SG_GROUPS
    for piece in range(x_ref.shape[0] // SG_PIECE_ROWS):
        r0 = piece * SG_PIECE_ROWS
        y = jnp.dot(x_ref[r0:r0 + SG_PIECE_ROWS, :], w_ref[...], preferred_element_type=F32)
        y = 0.5 * y * (1.0 + lax.erf(y * INV_SQRT2))
        u = y[:, 0:SG_WIDTH]
        vn_ref[piece] = _layer_norm(y[:, SG_WIDTH:], g_ref[...], b_ref[...]).astype(BF16)
        for g in range(SG_GROUPS):
            cols = slice(g * gd, (g + 1) * gd)
            vg = jnp.concatenate([vn_ref[piece, c * SG_CHUNK:(c + 1) * SG_CHUNK, cols]
                                  for c in range(chunks)], axis=1)
            mixed = jnp.dot(ws_ref[g], vg, preferred_element_type=F32)
            for c in range(chunks):
                mx = mixed[:, c * gd:(c + 1) * gd] + bs_ref[:, cols]
                rows = slice(r0 + c * SG_CHUNK, r0 + (c + 1) * SG_CHUNK)
                o_ref[rows, cols] = (u[c * SG_CHUNK:(c + 1) * SG_CHUNK, cols] * mx).astype(o_ref.dtype)


def _sg_branch(x, w, ln_g, ln_b, w_s, bs_full, *, tm=512):
    m, k = x.shape
    return pl.pallas_call(
        _sg_branch_kernel,
        grid=(m // tm,),
        in_specs=[pl.BlockSpec((tm, k), lambda i: (i, 0)),
                  _resident((k, 2 * SG_WIDTH)),
                  pl.BlockSpec((1, SG_WIDTH), lambda i: (0, 0)),
                  pl.BlockSpec((1, SG_WIDTH), lambda i: (0, 0)),
                  pl.BlockSpec(w_s.shape, lambda i: (0, 0, 0)),
                  pl.BlockSpec(bs_full.shape, lambda i: (0, 0))],
        out_specs=pl.BlockSpec((tm, SG_WIDTH), lambda i: (i, 0)),
        out_shape=jax.ShapeDtypeStruct((m, SG_WIDTH), BF16),
        scratch_shapes=[pltpu.VMEM((tm // SG_PIECE_ROWS, SG_PIECE_ROWS, SG_WIDTH), BF16)],
        compiler_params=_params("parallel"),
    )(x, w, ln_g, ln_b, w_s, bs_full)


FEAT_ROWS = 16
ONES_ROWS = 16
BEFORE, AFTER, DIAG = 0, 1, 2
NORM_SLACK = 1.02
BOUND_LIMIT = 70.0
HEADROOM = 60.0


def _pos_lo_bits(t):
    return max(0, t.bit_length() - 9)


def _feature_table(c, t):
    radix = float(1 << _pos_lo_bits(t))
    c1 = c.astype(BF16).astype(F32)
    c2 = (c - c1).astype(BF16).astype(F32)
    c3 = (c - c1 - c2).astype(BF16).astype(F32)
    nh = c.shape[0]
    z = jnp.zeros((nh, LANES), F32)
    ind = lambda lo: jnp.zeros((nh, LANES), F32).at[:, lo:lo + 3].set(1.0)
    pieces = jnp.stack([radix * c1, radix * c2, radix * c3, c1, c2, c3], axis=1)
    at = lambda lo: z.at[:, lo:lo + 6].set(pieces)
    rows = [ind(0), ind(3), ind(6), ind(9), at(12),
            at(6), at(0), ind(12), ind(15)]
    return jnp.stack(rows + [z] * (FEAT_ROWS - len(rows)), axis=1)


def _max_sq_norms(x_ref, groups=8):
    n = x_ref.shape[1] // groups
    g = jnp.square(x_ref[0, 0:n, :].astype(F32))
    for i in range(1, groups):
        g = jnp.maximum(g, jnp.square(x_ref[0, i * n:(i + 1) * n, :].astype(F32)))
    lane = lax.broadcasted_iota(jnp.int32, g.shape, 1)
    out = []
    for mask in (lane < QK_DIM, lane >= QK_DIM):
        n2 = jnp.sum(jnp.where(mask, g, 0.0), axis=1, keepdims=True)
        out.append(jnp.max(n2, axis=0, keepdims=True))
    return out


def _attn_kernel(c_ref, lq1_ref, lk1_ref, lq2_ref, lk2_ref, gain_ref, tab_ref,
                 q_ref, k_ref, v_ref, o_ref,
                 kf_ref, vt_ref, wq_ref, tbase_ref, tdiag_ref, mref_ref, m_ref, acc_ref, flag_ref,
                 *, t, lambda_init):
    h = pl.program_id(1)
    qi = pl.program_id(2)
    nk = k_ref.shape[1] // t
    c = c_ref[h]

    @pl.when((pl.program_id(0) == 0) & (h == 0) & (qi == 0))
    def _once():
        key = lax.broadcasted_iota(jnp.int32, (t, t), 0)
        qry = lax.broadcasted_iota(jnp.int32, (t, t), 1)
        tbase_ref[...] = jnp.abs(key - qry).astype(F32)

    @pl.when(qi == 0)
    def _per_head():
        tdiag_ref[...] = -c * tbase_ref[...]
        pos = lax.broadcasted_iota(jnp.int32, (t, LANES), 0)
        lo_bits = _pos_lo_bits(t)
        split = lambda x: ((x >> lo_bits).astype(F32), (x & ((1 << lo_bits) - 1)).astype(F32))
        hi, lo = split(pos)
        rhi, rlo = split(t - pos)
        tab = tab_ref[0]
        kfeat = (hi * tab[0:1] + lo * tab[1:2] + rhi * tab[2:3] + rlo * tab[3:4] + tab[4:5]).astype(BF16)
        q_before = -(tab[5:6] + hi * tab[7:8] + lo * tab[8:9])
        q_after = -(tab[6:7] + rhi * tab[7:8] + rlo * tab[8:9])
        kf_ref[:, 0:LANES] = k_ref[0]
        for j in range(nk):
            kf_ref[j * t:(j + 1) * t, LANES:2 * LANES] = kfeat
            vt_ref[j, 0:HEAD_DIM, :] = v_ref[0, j * t:(j + 1) * t, :].astype(F32).T.astype(BF16)
            vt_ref[j, HEAD_DIM:, :] = jnp.ones((ONES_ROWS, t), BF16)
        for half in range(2):
            rows = slice(half * t, (half + 1) * t)
            wq_ref[BEFORE, rows, LANES:2 * LANES] = q_before.astype(BF16)
            wq_ref[AFTER, rows, LANES:2 * LANES] = q_after.astype(BF16)
            wq_ref[DIAG, rows, LANES:2 * LANES] = jnp.zeros((t, LANES), BF16)
        qn = _max_sq_norms(q_ref)
        kn = _max_sq_norms(k_ref)
        bounds = [jnp.sqrt(a * b) * NORM_SLACK for a, b in zip(qn, kn)]
        for half in range(2):
            mref_ref[:, half * t:(half + 1) * t] = jnp.broadcast_to(bounds[half] - HEADROOM, (1, t))
        flag_ref[0] = (jnp.max(jnp.maximum(bounds[0], bounds[1])) <= BOUND_LIMIT).astype(jnp.int32)

    q = q_ref[0, pl.ds(pl.multiple_of(qi * t, t), t), :].astype(F32)
    lane = lax.broadcasted_iota(jnp.int32, q.shape, 1)
    q1 = jnp.where(lane < QK_DIM, q, 0.0).astype(BF16)
    q2 = jnp.where(lane >= QK_DIM, q, 0.0).astype(BF16)
    for variant in (BEFORE, AFTER, DIAG):
        wq_ref[variant, 0:t, 0:LANES] = q1
        wq_ref[variant, t:2 * t, 0:LANES] = q2
    fast_ok = flag_ref[0] == 1

    def place(r):
        j = qi + r
        after = j < nk
        return (jnp.where(after, j, j - nk), jnp.where(after, AFTER, BEFORE),
                jnp.where(after, r, nk - r))

    def block(r):
        jw, variant, dist = (qi, DIAG, 0) if isinstance(r, int) and r == 0 else place(r)
        kt = kf_ref[pl.ds(pl.multiple_of(jw * t, t), t), :]
        s = lax.dot_general(kt, wq_ref[variant], (((1,), (1,)), ((), ())),
                            preferred_element_type=F32)
        if isinstance(r, int) and r == 0:
            bias = tdiag_ref[...]
            return jnp.concatenate([s[:, 0:t] + bias, s[:, t:2 * t] + bias], axis=1), 0.0, vt_ref[jw]
        return s, -c * ((dist - 1) * t).astype(F32), vt_ref[jw]


    @pl.when(fast_ok)
    def _fixed_reference():
        m0 = mref_ref[...]
        for r in range(nk):
            s, shift, vt = block(r)
            pv = jnp.dot(vt, jnp.exp2(s - (m0 - shift)).astype(BF16), preferred_element_type=F32)
            if r == 0:
                acc_ref[...] = pv
            else:
                acc_ref[...] += pv

    @pl.when(jnp.logical_not(fast_ok))
    def _running_max():
        s, _, vt = block(0)
        m_ref[...] = jnp.max(s, axis=0, keepdims=True)
        acc_ref[...] = jnp.dot(vt, jnp.exp2(s - m_ref[...]).astype(BF16), preferred_element_type=F32)

        def fold(r, carry):
            s, shift, vt = block(r)
            m_old = m_ref[...]
            m_new = jnp.maximum(m_old, jnp.max(s, axis=0, keepdims=True) + shift)
            pv = jnp.dot(vt, jnp.exp2(s - (m_new - shift)).astype(BF16), preferred_element_type=F32)
            acc_ref[...] = jnp.exp2(m_old - m_new) * acc_ref[...] + pv
            m_ref[...] = m_new
            return carry

        lax.fori_loop(1, nk, fold, 0)

    lam = (jnp.exp(jnp.sum(lq1_ref[...] * lk1_ref[...], axis=-1, keepdims=True))
           - jnp.exp(jnp.sum(lq2_ref[...] * lk2_ref[...], axis=-1, keepdims=True))
           + lambda_init)
    rl = 1.0 / acc_ref[HEAD_DIM:HEAD_DIM + 1, :]
    o = (acc_ref[0:HEAD_DIM, 0:t] * rl[:, 0:t]
         - acc_ref[0:HEAD_DIM, t:2 * t] * (lam * rl[:, t:2 * t]))
    ms = jnp.mean(o * o, axis=0, keepdims=True)
    o = o * lax.rsqrt(ms + LN_EPS)
    o = o.T * gain_ref[...] * (1.0 - lambda_init)
    o_ref[0] = o.astype(o_ref.dtype)


def _attention(qkv, c, lq1, lk1, lq2, lk2, gain, *, batch, seq, lambda_init, t=1024):
    nh = N_HEADS
    vec = lambda n: pl.BlockSpec((1, n), lambda b, h, i: (0, 0))
    return pl.pallas_call(
        functools.partial(_attn_kernel, t=t, lambda_init=lambda_init),
        grid=(batch, nh, seq // t),
        in_specs=[pl.BlockSpec(memory_space=pltpu.SMEM),
                  vec(QK_DIM), vec(QK_DIM), vec(QK_DIM), vec(QK_DIM), vec(HEAD_DIM),
                  pl.BlockSpec((1, FEAT_ROWS, LANES), lambda b, h, i: (h, 0, 0)),
                  pl.BlockSpec((1, seq, HEAD_DIM), lambda b, h, i: (b, 0, h)),
                  pl.BlockSpec((1, seq, HEAD_DIM), lambda b, h, i: (b, 0, nh + h)),
                  pl.BlockSpec((1, seq, HEAD_DIM), lambda b, h, i: (b, 0, 2 * nh + h))],
        out_specs=pl.BlockSpec((1, t, HEAD_DIM), lambda b, h, i: (b, i, h)),
        out_shape=jax.ShapeDtypeStruct((batch, seq, ATTN_WIDTH), BF16),
        scratch_shapes=[pltpu.VMEM((seq, 2 * LANES), BF16),
                        pltpu.VMEM((seq // t, HEAD_DIM + ONES_ROWS, t), BF16),
                        pltpu.VMEM((3, 2 * t, 2 * LANES), BF16),
                        pltpu.VMEM((t, t), F32),
                        pltpu.VMEM((t, t), F32),
                        pltpu.VMEM((1, 2 * t), F32),
                        pltpu.VMEM((1, 2 * t), F32),
                        pltpu.VMEM((HEAD_DIM + ONES_ROWS, 2 * t), F32),
                        pltpu.SMEM((1,), jnp.int32)],
        compiler_params=_params("arbitrary", "arbitrary", "arbitrary"),
    )(c, lq1, lk1, lq2, lk2, gain, _feature_table(c, t), qkv, qkv, qkv)


def _merge_kernel(a_ref, s_ref, x_ref, wa_ref, ws_ref, wg0_ref, wg1_ref, bg0_ref, bg1_ref, o_ref):
    x = x_ref[...]
    g0 = jax.nn.sigmoid(jnp.dot(x, wg0_ref[...], preferred_element_type=F32) + bg0_ref[...])
    g1 = jax.nn.sigmoid(jnp.dot(x, wg1_ref[...], preferred_element_type=F32) + bg1_ref[...])
    ya = jnp.dot(a_ref[...], wa_ref[...], preferred_element_type=F32)
    ys = jnp.dot(s_ref[...], ws_ref[...], preferred_element_type=F32)
    o_ref[...] = (g0 * ya + g1 * ys).astype(o_ref.dtype)


def _merge(attn, sgo, x, w_gate, b_gate, w_ba, w_bs, *, tm=1024, tn=512):
    m, d = x.shape
    nb = d // tn
    return pl.pallas_call(
        _merge_kernel,
        grid=(m // tm, d // tn),
        in_specs=[pl.BlockSpec((tm, attn.shape[1]), lambda i, j: (i, 0)),
                  pl.BlockSpec((tm, sgo.shape[1]), lambda i, j: (i, 0)),
                  pl.BlockSpec((tm, d), lambda i, j: (i, 0)),
                  pl.BlockSpec((w_ba.shape[0], tn), lambda i, j: (0, j)),
                  pl.BlockSpec((w_bs.shape[0], tn), lambda i, j: (0, j)),
                  pl.BlockSpec((d, tn), lambda i, j: (0, j)),
                  pl.BlockSpec((d, tn), lambda i, j: (0, nb + j)),
                  pl.BlockSpec((1, tn), lambda i, j: (0, j)),
                  pl.BlockSpec((1, tn), lambda i, j: (0, nb + j))],
        out_specs=pl.BlockSpec((tm, tn), lambda i, j: (i, j)),
        out_shape=jax.ShapeDtypeStruct((m, d), BF16),
        compiler_params=_params("parallel", "arbitrary"),
    )(attn, sgo, x, w_ba, w_bs, w_gate, w_gate, b_gate, b_gate)


def _oproj_kernel(y_ref, w_ref, x_ref, g_ref, b_ref, o_ref, *, alpha):
    for r in range(y_ref.shape[0] // OPROJ_PIECE_ROWS):
        rows = slice(r * OPROJ_PIECE_ROWS, (r + 1) * OPROJ_PIECE_ROWS)
        y = jnp.dot(y_ref[rows, :], w_ref[...], preferred_element_type=F32)
        o_ref[rows, :] = _layer_norm(alpha * x_ref[rows, :] + y, g_ref[...], b_ref[...])


OPROJ_PIECE_ROWS = 256


def _oproj(merged, w_o, x, g, b, *, alpha, tm=1024):
    m, d = x.shape
    return pl.pallas_call(
        functools.partial(_oproj_kernel, alpha=alpha),
        grid=(m // tm,),
        in_specs=[pl.BlockSpec((tm, d), lambda i: (i, 0)),
                  _resident((d, d)),
                  pl.BlockSpec((tm, d), lambda i: (i, 0)),
                  pl.BlockSpec((1, d), lambda i: (0, 0)),
                  pl.BlockSpec((1, d), lambda i: (0, 0))],
        out_specs=pl.BlockSpec((tm, d), lambda i: (i, 0)),
        out_shape=jax.ShapeDtypeStruct((m, d), F32),
        compiler_params=_params("parallel"),
    )(merged, w_o, x, g, b)


def _mlp_kernel(x_ref, wu_ref, wd_ref, g_ref, b_ref, o_ref, xb_ref, *, alpha):
    f = pl.program_id(1)
    nf = pl.num_programs(1)

    def step(first, last):
        for r in range(x_ref.shape[0] // MLP_PIECE_ROWS):
            rows = slice(r * MLP_PIECE_ROWS, (r + 1) * MLP_PIECE_ROWS)
            if first:
                xb_ref[rows, :] = x_ref[rows, :].astype(BF16)
            hid = jnp.dot(xb_ref[rows, :], wu_ref[...], preferred_element_type=F32)
            hid = jnp.square(jnp.maximum(hid, 0.0)).astype(BF16)
            y = jnp.dot(hid, wd_ref[...], preferred_element_type=F32)
            if not first:
                y = o_ref[rows, :] + y
            if last:
                y = _layer_norm(alpha * x_ref[rows, :] + y, g_ref[...], b_ref[...])
            o_ref[rows, :] = y

    pl.when(f == 0)(functools.partial(step, True, False))
    pl.when((f > 0) & (f < nf - 1))(functools.partial(step, False, False))
    pl.when(f == nf - 1)(functools.partial(step, False, True))


MLP_PIECE_ROWS = 256


def _mlp(x, w_up, w_down, g, b, *, alpha, tm=512, tf=1024):
    m, d = x.shape
    dff = w_up.shape[1]
    return pl.pallas_call(
        functools.partial(_mlp_kernel, alpha=alpha),
        grid=(m // tm, dff // tf),
        in_specs=[pl.BlockSpec((tm, d), lambda i, f: (i, 0)),
                  pl.BlockSpec((d, tf), lambda i, f: (0, f)),
                  pl.BlockSpec((tf, d), lambda i, f: (f, 0)),
                  pl.BlockSpec((1, d), lambda i, f: (0, 0)),
                  pl.BlockSpec((1, d), lambda i, f: (0, 0))],
        out_specs=pl.BlockSpec((tm, d), lambda i, f: (i, 0)),
        out_shape=jax.ShapeDtypeStruct((m, d), F32),
        scratch_shapes=[pltpu.VMEM((tm, d), BF16)],
        compiler_params=_params("parallel", "arbitrary"),
    )(x, w_up, w_down, g, b)


def kernel(x, w_in, lambda_q1, lambda_k1, lambda_q2, lambda_k2, attn_subln_g, sg_ln_g, sg_ln_b, sg_w_s, sg_b_s, b_gate, w_branch_attn, w_branch_sg, w_o, ln1_g, ln1_b, w_up, w_down, ln2_g, ln2_b):
    batch, seq, d = x.shape
    depth = w_in.shape[0]
    m = batch * seq
    alpha = (2.0 * depth) ** 0.25
    k_off = ATTN_WIDTH
    u_off = 3 * ATTN_WIDTH
    gate_off = u_off + 2 * SG_WIDTH
    row = lambda p: p.reshape(1, -1).astype(F32)

    i = jnp.arange(1, N_HEADS + 1, dtype=F32)
    slopes = jnp.exp2(-8.0 * i / N_HEADS)
    col_scale = jnp.where(jnp.arange(u_off) < k_off, LOG2E * QK_DIM ** -0.5, 1.0).astype(F32)

    xf = x.reshape(m, d)
    for l in range(depth):
        lambda_init = 0.8 - 0.6 * math.exp(-0.3 * l)
        w_qkv = (w_in[l][:, :u_off] * col_scale).astype(BF16)
        w_sg = w_in[l][:, u_off:gate_off].astype(BF16)
        w_gate = w_in[l][:, gate_off:].astype(BF16)
        qkv, xb = _proj_cast(xf, w_qkv)
        gd = SG_WIDTH // SG_GROUPS
        bs_full = jnp.repeat(sg_b_s[l].T.astype(F32), gd, axis=1)
        sgo = _sg_branch(xb, w_sg, row(sg_ln_g[l]), row(sg_ln_b[l]), sg_w_s[l].astype(BF16), bs_full)

        attn = _attention(qkv.reshape(batch, seq, 3 * ATTN_WIDTH), LOG2E * slopes,
                          row(lambda_q1[l]), row(lambda_k1[l]), row(lambda_q2[l]), row(lambda_k2[l]),
                          row(attn_subln_g[l]), batch=batch, seq=seq, lambda_init=lambda_init)

        merged = _merge(attn.reshape(m, ATTN_WIDTH), sgo, xb, w_gate, row(b_gate[l]),
                        w_branch_attn[l].astype(BF16), w_branch_sg[l].astype(BF16))
        xf = _oproj(merged, w_o[l].astype(BF16), xf, row(ln1_g[l]), row(ln1_b[l]), alpha=alpha)
        xf = _mlp(xf, w_up[l].astype(BF16), w_down[l].astype(BF16), row(ln2_g[l]), row(ln2_b[l]), alpha=alpha)
    return xf.reshape(batch, seq, d)
```

```python
import functools
import math

import jax
import jax.numpy as jnp
from jax import lax
from jax.experimental import pallas as pl
from jax.experimental.pallas import tpu as pltpu

F32 = jnp.float32
BF16 = jnp.bfloat16

N_HEADS = 8
QK_DIM = 64
HEAD_DIM = 2 * QK_DIM
ATTN_WIDTH = N_HEADS * HEAD_DIM
SG_WIDTH = 1024
SG_GROUPS = 8
SG_CHUNK = 128
LN_EPS = 1e-5
INV_SQRT2 = 0.7071067811865476
LOG2E = 1.4426950408889634
LANES = 128

VMEM_LIMIT_BYTES = 60 * 1024 * 1024


def _params(*sem):
    return pltpu.CompilerParams(dimension_semantics=sem, vmem_limit_bytes=VMEM_LIMIT_BYTES)


def _layer_norm(y, g, b):
    mu = jnp.mean(y, axis=-1, keepdims=True)
    d = y - mu
    var = jnp.mean(d * d, axis=-1, keepdims=True)
    return d * lax.rsqrt(var + LN_EPS) * g + b


def _proj_cast_kernel(x_ref, w_ref, o_ref, xb_ref):
    xb = x_ref[...].astype(BF16)
    xb_ref[...] = xb
    o_ref[...] = jnp.dot(xb, w_ref[...], preferred_element_type=F32).astype(o_ref.dtype)


def _resident(shape):
    return pl.BlockSpec(shape, lambda i: (0,) * len(shape), pipeline_mode=pl.Buffered(1))


def _proj_cast(x, w, *, tm=512):
    m, k = x.shape
    n = w.shape[1]
    return pl.pallas_call(
        _proj_cast_kernel,
        grid=(m // tm,),
        in_specs=[pl.BlockSpec((tm, k), lambda i: (i, 0)), _resident((k, n))],
        out_specs=[pl.BlockSpec((tm, n), lambda i: (i, 0)),
                   pl.BlockSpec((tm, k), lambda i: (i, 0))],
        out_shape=[jax.ShapeDtypeStruct((m, n), BF16),
                   jax.ShapeDtypeStruct((m, k), BF16)],
        compiler_params=_params("parallel"),
    )(x, w)


SG_PIECE_ROWS = 256


def _sg_branch_kernel(x_ref, w_ref, g_ref, b_ref, ws_ref, bs_ref, o_ref, vn_ref):
    chunks = SG_PIECE_ROWS // SG_CHUNK
    gd = SG_WIDTH // SG_GROUPS
    for piece in range(x_ref.shape[0] // SG_PIECE_ROWS):
        r0 = piece * SG_PIECE_ROWS
        y = jnp.dot(x_ref[r0:r0 + SG_PIECE_ROWS, :], w_ref[...], preferred_element_type=F32)
        y = 0.5 * y * (1.0 + lax.erf(y * INV_SQRT2))
        u = y[:, 0:SG_WIDTH]
        vn_ref[piece] = _layer_norm(y[:, SG_WIDTH:], g_ref[...], b_ref[...]).astype(BF16)
        for g in range(SG_GROUPS):
            cols = slice(g * gd, (g + 1) * gd)
            vg = jnp.concatenate([vn_ref[piece, c * SG_CHUNK:(c + 1) * SG_CHUNK, cols]
                                  for c in range(chunks)], axis=1)
            mixed = jnp.dot(ws_ref[g], vg, preferred_element_type=F32)
            for c in range(chunks):
                mx = mixed[:, c * gd:(c + 1) * gd] + bs_ref[:, cols]
                rows = slice(r0 + c * SG_CHUNK, r0 + (c + 1) * SG_CHUNK)
                o_ref[rows, cols] = (u[c * SG_CHUNK:(c + 1) * SG_CHUNK, cols] * mx).astype(o_ref.dtype)


def _sg_branch(x, w, ln_g, ln_b, w_s, bs_full, *, tm=512):
    m, k = x.shape
    return pl.pallas_call(
        _sg_branch_kernel,
        grid=(m // tm,),
        in_specs=[pl.BlockSpec((tm, k), lambda i: (i, 0)),
                  _resident((k, 2 * SG_WIDTH)),
                  pl.BlockSpec((1, SG_WIDTH), lambda i: (0, 0)),
                  pl.BlockSpec((1, SG_WIDTH), lambda i: (0, 0)),
                  pl.BlockSpec(w_s.shape, lambda i: (0, 0, 0)),
                  pl.BlockSpec(bs_full.shape, lambda i: (0, 0))],
        out_specs=pl.BlockSpec((tm, SG_WIDTH), lambda i: (i, 0)),
        out_shape=jax.ShapeDtypeStruct((m, SG_WIDTH), BF16),
        scratch_shapes=[pltpu.VMEM((tm // SG_PIECE_ROWS, SG_PIECE_ROWS, SG_WIDTH), BF16)],
        compiler_params=_params("parallel"),
    )(x, w, ln_g, ln_b, w_s, bs_full)


FEAT_ROWS = 16
ONES_ROWS = 16
BEFORE, AFTER, DIAG = 0, 1, 2
NORM_SLACK = 1.02
BOUND_LIMIT = 70.0
HEADROOM = 60.0
ZERO_EXPONENT = 190.0


def _pos_lo_bits(t):
    return max(0, t.bit_length() - 9)


def _feature_table(c, t):
    radix = float(1 << _pos_lo_bits(t))
    c1 = c.astype(BF16).astype(F32)
    c2 = (c - c1).astype(BF16).astype(F32)
    c3 = (c - c1 - c2).astype(BF16).astype(F32)
    nh = c.shape[0]
    z = jnp.zeros((nh, LANES), F32)
    ind = lambda lo: jnp.zeros((nh, LANES), F32).at[:, lo:lo + 3].set(1.0)
    pieces = jnp.stack([radix * c1, radix * c2, radix * c3, c1, c2, c3], axis=1)
    at = lambda lo: z.at[:, lo:lo + 6].set(pieces)
    rows = [ind(0), ind(3), ind(6), ind(9), at(12),
            at(6), at(0), ind(12), ind(15)]
    return jnp.stack(rows + [z] * (FEAT_ROWS - len(rows)), axis=1)


def _near_rows(c, t, nk):
    rows = -(-int(math.ceil((HEADROOM + ZERO_EXPONENT) / c - 1.0)) // LANES) * LANES
    return rows if rows < t and nk >= 3 else None


def _max_sq_norms(x_ref, groups=8):
    n = x_ref.shape[1] // groups
    g = jnp.square(x_ref[0, 0:n, :].astype(F32))
    for i in range(1, groups):
        g = jnp.maximum(g, jnp.square(x_ref[0, i * n:(i + 1) * n, :].astype(F32)))
    lane = lax.broadcasted_iota(jnp.int32, g.shape, 1)
    out = []
    for mask in (lane < QK_DIM, lane >= QK_DIM):
        n2 = jnp.sum(jnp.where(mask, g, 0.0), axis=1, keepdims=True)
        out.append(jnp.max(n2, axis=0, keepdims=True))
    return out


def _attn_kernel(c_ref, lq1_ref, lk1_ref, lq2_ref, lk2_ref, gain_ref, tab_ref,
                 q_ref, k_ref, v_ref, o_ref,
                 kf_ref, vt_ref, wq_ref, tbase_ref, tdiag_ref, mref_ref, m_ref, acc_ref, flag_ref,
                 *, t, lambda_init, head_near_rows):
    h = pl.program_id(0)
    bi = pl.program_id(1)
    qi = pl.program_id(2)
    nk = k_ref.shape[1] // t
    c = c_ref[h]

    @pl.when((h == 0) & (bi == 0) & (qi == 0))
    def _once():
        key = lax.broadcasted_iota(jnp.int32, (t, t), 0)
        qry = lax.broadcasted_iota(jnp.int32, (t, t), 1)
        tbase_ref[...] = jnp.abs(key - qry).astype(F32)

    @pl.when((bi == 0) & (qi == 0))
    def _per_head():
        tdiag_ref[...] = -c * tbase_ref[...]
        pos = lax.broadcasted_iota(jnp.int32, (t, LANES), 0)
        lo_bits = _pos_lo_bits(t)
        split = lambda x: ((x >> lo_bits).astype(F32), (x & ((1 << lo_bits) - 1)).astype(F32))
        hi, lo = split(pos)
        rhi, rlo = split(t - pos)
        tab = tab_ref[0]
        kfeat = (hi * tab[0:1] + lo * tab[1:2] + rhi * tab[2:3] + rlo * tab[3:4] + tab[4:5]).astype(BF16)
        q_before = -(tab[5:6] + hi * tab[7:8] + lo * tab[8:9])
        q_after = -(tab[6:7] + rhi * tab[7:8] + rlo * tab[8:9])
        for j in range(nk):
            kf_ref[j * t:(j + 1) * t, LANES:2 * LANES] = kfeat
            vt_ref[j, HEAD_DIM:, :] = jnp.ones((ONES_ROWS, t), BF16)
        for half in range(2):
            rows = slice(half * t, (half + 1) * t)
            wq_ref[BEFORE, rows, LANES:2 * LANES] = q_before.astype(BF16)
            wq_ref[AFTER, rows, LANES:2 * LANES] = q_after.astype(BF16)
            wq_ref[DIAG, rows, LANES:2 * LANES] = jnp.zeros((t, LANES), BF16)

    @pl.when(qi == 0)
    def _per_batch_and_head():
        kf_ref[:, 0:LANES] = k_ref[0]
        for j in range(nk):
            vt_ref[j, 0:HEAD_DIM, :] = v_ref[0, j * t:(j + 1) * t, :].astype(F32).T.astype(BF16)
        qn = _max_sq_norms(q_ref)
        kn = _max_sq_norms(k_ref)
        bounds = [jnp.sqrt(a * b) * NORM_SLACK for a, b in zip(qn, kn)]
        for half in range(2):
            mref_ref[:, half * t:(half + 1) * t] = jnp.broadcast_to(bounds[half] - HEADROOM, (1, t))
        flag_ref[0] = (jnp.max(jnp.maximum(bounds[0], bounds[1])) <= BOUND_LIMIT).astype(jnp.int32)

    q = q_ref[0, pl.ds(pl.multiple_of(qi * t, t), t), :].astype(F32)
    lane = lax.broadcasted_iota(jnp.int32, q.shape, 1)
    q1 = jnp.where(lane < QK_DIM, q, 0.0).astype(BF16)
    q2 = jnp.where(lane >= QK_DIM, q, 0.0).astype(BF16)
    for variant in (BEFORE, AFTER, DIAG):
        wq_ref[variant, 0:t, 0:LANES] = q1
        wq_ref[variant, t:2 * t, 0:LANES] = q2
    fast_ok = flag_ref[0] == 1

    def place(r):
        j = qi + r
        after = j < nk
        return (jnp.where(after, j, j - nk), jnp.where(after, AFTER, BEFORE),
                jnp.where(after, r, nk - r))

    def block(r, rows=(0, t)):
        row0, n = rows
        jw, variant, dist = (qi, DIAG, 0) if isinstance(r, int) and r == 0 else place(r)
        kt = kf_ref[pl.ds(pl.multiple_of(jw * t + row0, LANES), n), :]
        s = lax.dot_general(kt, wq_ref[variant], (((1,), (1,)), ((), ())),
                            preferred_element_type=F32)
        vt = vt_ref[jw, :, row0:row0 + n]
        if isinstance(r, int) and r == 0:
            bias = tdiag_ref[row0:row0 + n, :]
            return jnp.concatenate([s[:, 0:t] + bias, s[:, t:2 * t] + bias], axis=1), 0.0, vt
        return s, -c * ((dist - 1) * t).astype(F32), vt


    def fixed_reference(near_rows):
        if near_rows is None:
            visits = [(r, (0, t)) for r in range(nk)]
        else:
            visits = [(0, (0, t)), (1, (0, near_rows)), (nk - 1, (t - near_rows, near_rows))]
        m0 = mref_ref[...]
        for i, (r, rows) in enumerate(visits):
            s, shift, vt = block(r, rows)
            pv = jnp.dot(vt, jnp.exp2(s - (m0 - shift)).astype(BF16), preferred_element_type=F32)
            if i == 0:
                acc_ref[...] = pv
            else:
                acc_ref[...] += pv

    for near_rows in sorted(set(head_near_rows), key=lambda v: (v is None, v)):
        in_class = functools.reduce(jnp.logical_or,
                                    [h == hh for hh, v in enumerate(head_near_rows) if v == near_rows])
        pl.when(fast_ok & in_class)(functools.partial(fixed_reference, near_rows))

    @pl.when(jnp.logical_not(fast_ok))
    def _running_max():
        s, _, vt = block(0)
        m_ref[...] = jnp.max(s, axis=0, keepdims=True)
        acc_ref[...] = jnp.dot(vt, jnp.exp2(s - m_ref[...]).astype(BF16), preferred_element_type=F32)

        def fold(r, carry):
            s, shift, vt = block(r)
            m_old = m_ref[...]
            m_new = jnp.maximum(m_old, jnp.max(s, axis=0, keepdims=True) + shift)
            pv = jnp.dot(vt, jnp.exp2(s - (m_new - shift)).astype(BF16), preferred_element_type=F32)
            acc_ref[...] = jnp.exp2(m_old - m_new) * acc_ref[...] + pv
            m_ref[...] = m_new
            return carry

        lax.fori_loop(1, nk, fold, 0)

    lam = (jnp.exp(jnp.sum(lq1_ref[...] * lk1_ref[...], axis=-1, keepdims=True))
           - jnp.exp(jnp.sum(lq2_ref[...] * lk2_ref[...], axis=-1, keepdims=True))
           + lambda_init)
    rl = 1.0 / acc_ref[HEAD_DIM:HEAD_DIM + 1, :]
    o = (acc_ref[0:HEAD_DIM, 0:t] * rl[:, 0:t]
         - acc_ref[0:HEAD_DIM, t:2 * t] * (lam * rl[:, t:2 * t]))
    ms = jnp.mean(o * o, axis=0, keepdims=True)
    o = o * lax.rsqrt(ms + LN_EPS)
    o = o.T * gain_ref[...] * (1.0 - lambda_init)
    o_ref[0] = o.astype(o_ref.dtype)


def _attention(qkv, c, lq1, lk1, lq2, lk2, gain, *, batch, seq, lambda_init, t=1024):
    nh = N_HEADS
    vec = lambda n: pl.BlockSpec((1, n), lambda h, b, i: (0, 0))
    near = tuple(_near_rows(LOG2E * 2.0 ** (-8.0 * (i + 1) / nh), t, seq // t) for i in range(nh))
    return pl.pallas_call(
        functools.partial(_attn_kernel, t=t, lambda_init=lambda_init, head_near_rows=near),
        grid=(nh, batch, seq // t),
        in_specs=[pl.BlockSpec(memory_space=pltpu.SMEM),
                  vec(QK_DIM), vec(QK_DIM), vec(QK_DIM), vec(QK_DIM), vec(HEAD_DIM),
                  pl.BlockSpec((1, FEAT_ROWS, LANES), lambda h, b, i: (h, 0, 0)),
                  pl.BlockSpec((1, seq, HEAD_DIM), lambda h, b, i: (b, 0, h)),
                  pl.BlockSpec((1, seq, HEAD_DIM), lambda h, b, i: (b, 0, nh + h)),
                  pl.BlockSpec((1, seq, HEAD_DIM), lambda h, b, i: (b, 0, 2 * nh + h))],
        out_specs=pl.BlockSpec((1, t, HEAD_DIM), lambda h, b, i: (b, i, h)),
        out_shape=jax.ShapeDtypeStruct((batch, seq, ATTN_WIDTH), BF16),
        scratch_shapes=[pltpu.VMEM((seq, 2 * LANES), BF16),
                        pltpu.VMEM((seq // t, HEAD_DIM + ONES_ROWS, t), BF16),
                        pltpu.VMEM((3, 2 * t, 2 * LANES), BF16),
                        pltpu.VMEM((t, t), F32),
                        pltpu.VMEM((t, t), F32),
                        pltpu.VMEM((1, 2 * t), F32),
                        pltpu.VMEM((1, 2 * t), F32),
                        pltpu.VMEM((HEAD_DIM + ONES_ROWS, 2 * t), F32),
                        pltpu.SMEM((1,), jnp.int32)],
        compiler_params=_params("arbitrary", "arbitrary", "arbitrary"),
    )(c, lq1, lk1, lq2, lk2, gain, _feature_table(c, t), qkv, qkv, qkv)


def _merge_kernel(a_ref, s_ref, x_ref, wa_ref, ws_ref, wg0_ref, wg1_ref, bg0_ref, bg1_ref, o_ref):
    x = x_ref[...]
    g0 = jax.nn.sigmoid(jnp.dot(x, wg0_ref[...], preferred_element_type=F32) + bg0_ref[...])
    g1 = jax.nn.sigmoid(jnp.dot(x, wg1_ref[...], preferred_element_type=F32) + bg1_ref[...])
    ya = jnp.dot(a_ref[...], wa_ref[...], preferred_element_type=F32)
    ys = jnp.dot(s_ref[...], ws_ref[...], preferred_element_type=F32)
    o_ref[...] = (g0 * ya + g1 * ys).astype(o_ref.dtype)


def _merge(attn, sgo, x, w_gate, b_gate, w_ba, w_bs, *, tm=1024, tn=512):
    m, d = x.shape
    nb = d // tn
    return pl.pallas_call(
        _merge_kernel,
        grid=(m // tm, d // tn),
        in_specs=[pl.BlockSpec((tm, attn.shape[1]), lambda i, j: (i, 0)),
                  pl.BlockSpec((tm, sgo.shape[1]), lambda i, j: (i, 0)),
                  pl.BlockSpec((tm, d), lambda i, j: (i, 0)),
                  pl.BlockSpec((w_ba.shape[0], tn), lambda i, j: (0, j)),
                  pl.BlockSpec((w_bs.shape[0], tn), lambda i, j: (0, j)),
                  pl.BlockSpec((d, tn), lambda i, j: (0, j)),
                  pl.BlockSpec((d, tn), lambda i, j: (0, nb + j)),
                  pl.BlockSpec((1, tn), lambda i, j: (0, j)),
                  pl.BlockSpec((1, tn), lambda i, j: (0, nb + j))],
        out_specs=pl.BlockSpec((tm, tn), lambda i, j: (i, j)),
        out_shape=jax.ShapeDtypeStruct((m, d), BF16),
        compiler_params=_params("parallel", "arbitrary"),
    )(attn, sgo, x, w_ba, w_bs, w_gate, w_gate, b_gate, b_gate)


def _oproj_kernel(y_ref, w_ref, x_ref, g_ref, b_ref, o_ref, *, alpha):
    for r in range(y_ref.shape[0] // OPROJ_PIECE_ROWS):
        rows = slice(r * OPROJ_PIECE_ROWS, (r + 1) * OPROJ_PIECE_ROWS)
        y = jnp.dot(y_ref[rows, :], w_ref[...], preferred_element_type=F32)
        o_ref[rows, :] = _layer_norm(alpha * x_ref[rows, :] + y, g_ref[...], b_ref[...])


OPROJ_PIECE_ROWS = 256


def _oproj(merged, w_o, x, g, b, *, alpha, tm=1024):
    m, d = x.shape
    return pl.pallas_call(
        functools.partial(_oproj_kernel, alpha=alpha),
        grid=(m // tm,),
        in_specs=[pl.BlockSpec((tm, d), lambda i: (i, 0)),
                  _resident((d, d)),
                  pl.BlockSpec((tm, d), lambda i: (i, 0)),
                  pl.BlockSpec((1, d), lambda i: (0, 0)),
                  pl.BlockSpec((1, d), lambda i: (0, 0))],
        out_specs=pl.BlockSpec((tm, d), lambda i: (i, 0)),
        out_shape=jax.ShapeDtypeStruct((m, d), F32),
        compiler_params=_params("parallel"),
    )(merged, w_o, x, g, b)


def _mlp_kernel(x_ref, wu_ref, wd_ref, g_ref, b_ref, o_ref, xb_ref, *, alpha):
    f = pl.program_id(1)
    nf = pl.num_programs(1)

    def step(first, last):
        for r in range(x_ref.shape[0] // MLP_PIECE_ROWS):
            rows = slice(r * MLP_PIECE_ROWS, (r + 1) * MLP_PIECE_ROWS)
            if first:
                xb_ref[rows, :] = x_ref[rows, :].astype(BF16)
            hid = jnp.dot(xb_ref[rows, :], wu_ref[...], preferred_element_type=F32)
            hid = jnp.square(jnp.maximum(hid, 0.0)).astype(BF16)
            y = jnp.dot(hid, wd_ref[...], preferred_element_type=F32)
            if not first:
                y = o_ref[rows, :] + y
            if last:
                y = _layer_norm(alpha * x_ref[rows, :] + y, g_ref[...], b_ref[...])
            o_ref[rows, :] = y

    pl.when(f == 0)(functools.partial(step, True, False))
    pl.when((f > 0) & (f < nf - 1))(functools.partial(step, False, False))
    pl.when(f == nf - 1)(functools.partial(step, False, True))


MLP_PIECE_ROWS = 256


def _mlp(x, w_up, w_down, g, b, *, alpha, tm=512, tf=1024):
    m, d = x.shape
    dff = w_up.shape[1]
    return pl.pallas_call(
        functools.partial(_mlp_kernel, alpha=alpha),
        grid=(m // tm, dff // tf),
        in_specs=[pl.BlockSpec((tm, d), lambda i, f: (i, 0)),
                  pl.BlockSpec((d, tf), lambda i, f: (0, f)),
                  pl.BlockSpec((tf, d), lambda i, f: (f, 0)),
                  pl.BlockSpec((1, d), lambda i, f: (0, 0)),
                  pl.BlockSpec((1, d), lambda i, f: (0, 0))],
        out_specs=pl.BlockSpec((tm, d), lambda i, f: (i, 0)),
        out_shape=jax.ShapeDtypeStruct((m, d), F32),
        scratch_shapes=[pltpu.VMEM((tm, d), BF16)],
        compiler_params=_params("parallel", "arbitrary"),
    )(x, w_up, w_down, g, b)


def kernel(x, w_in, lambda_q1, lambda_k1, lambda_q2, lambda_k2, attn_subln_g, sg_ln_g, sg_ln_b, sg_w_s, sg_b_s, b_gate, w_branch_attn, w_branch_sg, w_o, ln1_g, ln1_b, w_up, w_down, ln2_g, ln2_b):
    batch, seq, d = x.shape
    depth = w_in.shape[0]
    m = batch * seq
    alpha = (2.0 * depth) ** 0.25
    k_off = ATTN_WIDTH
    u_off = 3 * ATTN_WIDTH
    gate_off = u_off + 2 * SG_WIDTH
    row = lambda p: p.reshape(1, -1).astype(F32)

    i = jnp.arange(1, N_HEADS + 1, dtype=F32)
    slopes = jnp.exp2(-8.0 * i / N_HEADS)
    col_scale = jnp.where(jnp.arange(u_off) < k_off, LOG2E * QK_DIM ** -0.5, 1.0).astype(F32)

    xf = x.reshape(m, d)
    for l in range(depth):
        lambda_init = 0.8 - 0.6 * math.exp(-0.3 * l)
        w_qkv = (w_in[l][:, :u_off] * col_scale).astype(BF16)
        w_sg = w_in[l][:, u_off:gate_off].astype(BF16)
        w_gate = w_in[l][:, gate_off:].astype(BF16)
        qkv, xb = _proj_cast(xf, w_qkv)
        gd = SG_WIDTH // SG_GROUPS
        bs_full = jnp.repeat(sg_b_s[l].T.astype(F32), gd, axis=1)
        sgo = _sg_branch(xb, w_sg, row(sg_ln_g[l]), row(sg_ln_b[l]), sg_w_s[l].astype(BF16), bs_full)

        attn = _attention(qkv.reshape(batch, seq, 3 * ATTN_WIDTH), LOG2E * slopes,
                          row(lambda_q1[l]), row(lambda_k1[l]), row(lambda_q2[l]), row(lambda_k2[l]),
                          row(attn_subln_g[l]), batch=batch, seq=seq, lambda_init=lambda_init)

        merged = _merge(attn.reshape(m, ATTN_WIDTH), sgo, xb, w_gate, row(b_gate[l]),
                        w_branch_attn[l].astype(BF16), w_branch_sg[l].astype(BF16))
        xf = _oproj(merged, w_o[l].astype(BF16), xf, row(ln1_g[l]), row(ln1_b[l]), alpha=alpha)
        xf = _mlp(xf, w_up[l].astype(BF16), w_down[l].astype(BF16), row(ln2_g[l]), row(ln2_b[l]), alpha=alpha)
    return xf.reshape(batch, seq, d)
```

```python
import functools
import math

import jax
import jax.numpy as jnp
from jax import lax
from jax.experimental import pallas as pl
from jax.experimental.pallas import tpu as pltpu

F32 = jnp.float32
BF16 = jnp.bfloat16

N_HEADS = 8
QK_DIM = 64
HEAD_DIM = 2 * QK_DIM
ATTN_WIDTH = N_HEADS * HEAD_DIM
SG_WIDTH = 1024
SG_GROUPS = 8
SG_CHUNK = 128
LN_EPS = 1e-5
INV_SQRT2 = 0.7071067811865476
LOG2E = 1.4426950408889634
LANES = 128

VMEM_LIMIT_BYTES = 60 * 1024 * 1024


def _params(*sem):
    return pltpu.CompilerParams(dimension_semantics=sem, vmem_limit_bytes=VMEM_LIMIT_BYTES)


def _layer_norm(y, g, b):
    mu = jnp.mean(y, axis=-1, keepdims=True)
    d = y - mu
    var = jnp.mean(d * d, axis=-1, keepdims=True)
    return d * lax.rsqrt(var + LN_EPS) * g + b


def _proj_cast_kernel(x_ref, w_ref, o_ref, xb_ref):
    xb = x_ref[...].astype(BF16)
    xb_ref[...] = xb
    o_ref[...] = jnp.dot(xb, w_ref[...], preferred_element_type=F32).astype(o_ref.dtype)


def _resident(shape, index=None):
    index = (0,) * len(shape) if index is None else index
    return pl.BlockSpec(shape, lambda i: index, pipeline_mode=pl.Buffered(1))


def _proj_cast(x, w, n, *, tm=512):
    m, k = x.shape
    return pl.pallas_call(
        _proj_cast_kernel,
        grid=(m // tm,),
        in_specs=[pl.BlockSpec((tm, k), lambda i: (i, 0)), _resident((k, n))],
        out_specs=[pl.BlockSpec((tm, n), lambda i: (i, 0)),
                   pl.BlockSpec((tm, k), lambda i: (i, 0))],
        out_shape=[jax.ShapeDtypeStruct((m, n), BF16),
                   jax.ShapeDtypeStruct((m, k), BF16)],
        compiler_params=_params("parallel"),
    )(x, w)


SG_PIECE_ROWS = 256


def _sg_branch_kernel(x_ref, wu_ref, wv_ref, g_ref, b_ref, ws_ref, bs_ref, o_ref, u_ref, vn_ref):
    gelu = lambda y: 0.5 * y * (1.0 + lax.erf(y * INV_SQRT2))
    pieces = x_ref.shape[0] // SG_PIECE_ROWS
    chunks = SG_PIECE_ROWS // SG_CHUNK
    gd = SG_WIDTH // SG_GROUPS
    for piece in range(pieces):
        rows = slice(piece * SG_PIECE_ROWS, (piece + 1) * SG_PIECE_ROWS)
        u_ref[rows, :] = gelu(jnp.dot(x_ref[rows, :], wu_ref[...], preferred_element_type=F32))
        v = gelu(jnp.dot(x_ref[rows, :], wv_ref[...], preferred_element_type=F32))
        vn_ref[rows, :] = _layer_norm(v, g_ref[...], b_ref[...]).astype(BF16)
    for piece in range(pieces):
        r0 = piece * SG_PIECE_ROWS
        for g in range(SG_GROUPS):
            cols = slice(g * gd, (g + 1) * gd)
            vg = jnp.concatenate([vn_ref[r0 + c * SG_CHUNK:r0 + (c + 1) * SG_CHUNK, cols]
                                  for c in range(chunks)], axis=1)
            mixed = jnp.dot(ws_ref[g], vg, preferred_element_type=F32)
            for c in range(chunks):
                mx = mixed[:, c * gd:(c + 1) * gd] + bs_ref[:, cols]
                rows = slice(r0 + c * SG_CHUNK, r0 + (c + 1) * SG_CHUNK)
                o_ref[rows, cols] = (u_ref[rows, cols] * mx).astype(o_ref.dtype)


def _sg_branch(x, w_in, u_off, ln_g, ln_b, w_s, bs_full, *, tm=512):
    m, k = x.shape
    ju = u_off // SG_WIDTH
    return pl.pallas_call(
        _sg_branch_kernel,
        grid=(m // tm,),
        in_specs=[pl.BlockSpec((tm, k), lambda i: (i, 0)),
                  _resident((k, SG_WIDTH), (0, ju)),
                  _resident((k, SG_WIDTH), (0, ju + 1)),
                  pl.BlockSpec((1, SG_WIDTH), lambda i: (0, 0)),
                  pl.BlockSpec((1, SG_WIDTH), lambda i: (0, 0)),
                  pl.BlockSpec(w_s.shape, lambda i: (0, 0, 0)),
                  pl.BlockSpec(bs_full.shape, lambda i: (0, 0))],
        out_specs=pl.BlockSpec((tm, SG_WIDTH), lambda i: (i, 0)),
        out_shape=jax.ShapeDtypeStruct((m, SG_WIDTH), BF16),
        scratch_shapes=[pltpu.VMEM((tm, SG_WIDTH), F32), pltpu.VMEM((tm, SG_WIDTH), BF16)],
        compiler_params=_params("parallel"),
    )(x, w_in, w_in, ln_g, ln_b, w_s, bs_full)


FEAT_ROWS = 16
ONES_ROWS = 16
BEFORE, AFTER, DIAG = 0, 1, 2
NORM_SLACK = 1.02
BOUND_LIMIT = 70.0
HEADROOM = 60.0
ZERO_EXPONENT = 190.0


def _pos_lo_bits(t):
    return max(0, t.bit_length() - 9)


def _feature_table(c, t):
    radix = float(1 << _pos_lo_bits(t))
    c1 = c.astype(BF16).astype(F32)
    c2 = (c - c1).astype(BF16).astype(F32)
    c3 = (c - c1 - c2).astype(BF16).astype(F32)
    nh = c.shape[0]
    z = jnp.zeros((nh, LANES), F32)
    ind = lambda lo: jnp.zeros((nh, LANES), F32).at[:, lo:lo + 3].set(1.0)
    pieces = jnp.stack([radix * c1, radix * c2, radix * c3, c1, c2, c3], axis=1)
    at = lambda lo: z.at[:, lo:lo + 6].set(pieces)
    rows = [ind(0), ind(3), ind(6), ind(9), at(12),
            at(6), at(0), ind(12), ind(15)]
    return jnp.stack(rows + [z] * (FEAT_ROWS - len(rows)), axis=1)


def _near_rows(c, t, nk):
    rows = -(-int(math.ceil((HEADROOM + ZERO_EXPONENT) / c - 1.0)) // LANES) * LANES
    return rows if rows < t and nk >= 3 else None


def _max_sq_norms(x_ref, groups=8):
    n = x_ref.shape[1] // groups
    g = jnp.square(x_ref[0, 0:n, :].astype(F32))
    for i in range(1, groups):
        g = jnp.maximum(g, jnp.square(x_ref[0, i * n:(i + 1) * n, :].astype(F32)))
    lane = lax.broadcasted_iota(jnp.int32, g.shape, 1)
    out = []
    for mask in (lane < QK_DIM, lane >= QK_DIM):
        n2 = jnp.sum(jnp.where(mask, g, 0.0), axis=1, keepdims=True)
        out.append(jnp.max(n2, axis=0, keepdims=True))
    return out


def _attn_kernel(c_ref, lq1_ref, lk1_ref, lq2_ref, lk2_ref, gain_ref, tab_ref,
                 q_ref, k_ref, v_ref, o_ref,
                 kf_ref, vt_ref, wq_ref, tbase_ref, tdiag_ref, mref_ref, m_ref, acc_ref, flag_ref,
                 *, t, lambda_init, head_near_rows):
    h = pl.program_id(0)
    bi = pl.program_id(1)
    qi = pl.program_id(2)
    nk = k_ref.shape[1] // t
    c = c_ref[h]

    @pl.when((h == 0) & (bi == 0) & (qi == 0))
    def _once():
        key = lax.broadcasted_iota(jnp.int32, (t, t), 0)
        qry = lax.broadcasted_iota(jnp.int32, (t, t), 1)
        tbase_ref[...] = jnp.abs(key - qry).astype(F32)

    @pl.when((bi == 0) & (qi == 0))
    def _per_head():
        tdiag_ref[...] = -c * tbase_ref[...]
        pos = lax.broadcasted_iota(jnp.int32, (t, LANES), 0)
        lo_bits = _pos_lo_bits(t)
        split = lambda x: ((x >> lo_bits).astype(F32), (x & ((1 << lo_bits) - 1)).astype(F32))
        hi, lo = split(pos)
        rhi, rlo = split(t - pos)
        tab = tab_ref[0]
        kfeat = (hi * tab[0:1] + lo * tab[1:2] + rhi * tab[2:3] + rlo * tab[3:4] + tab[4:5]).astype(BF16)
        q_before = -(tab[5:6] + hi * tab[7:8] + lo * tab[8:9])
        q_after = -(tab[6:7] + rhi * tab[7:8] + rlo * tab[8:9])
        for j in range(nk):
            kf_ref[j * t:(j + 1) * t, LANES:2 * LANES] = kfeat
            vt_ref[j, HEAD_DIM:, :] = jnp.ones((ONES_ROWS, t), BF16)
        for half in range(2):
            rows = slice(half * t, (half + 1) * t)
            wq_ref[BEFORE, rows, LANES:2 * LANES] = q_before.astype(BF16)
            wq_ref[AFTER, rows, LANES:2 * LANES] = q_after.astype(BF16)
            wq_ref[DIAG, rows, LANES:2 * LANES] = jnp.zeros((t, LANES), BF16)

    @pl.when(qi == 0)
    def _per_batch_and_head():
        kf_ref[:, 0:LANES] = k_ref[0]
        for j in range(nk):
            vt_ref[j, 0:HEAD_DIM, :] = v_ref[0, j * t:(j + 1) * t, :].astype(F32).T.astype(BF16)
        qn = _max_sq_norms(q_ref)
        kn = _max_sq_norms(k_ref)
        bounds = [jnp.sqrt(a * b) * NORM_SLACK for a, b in zip(qn, kn)]
        for half in range(2):
            mref_ref[:, half * t:(half + 1) * t] = jnp.broadcast_to(bounds[half] - HEADROOM, (1, t))
        flag_ref[0] = (jnp.max(jnp.maximum(bounds[0], bounds[1])) <= BOUND_LIMIT).astype(jnp.int32)

    q = q_ref[0, pl.ds(pl.multiple_of(qi * t, t), t), :].astype(F32)
    lane = lax.broadcasted_iota(jnp.int32, q.shape, 1)
    q1 = jnp.where(lane < QK_DIM, q, 0.0).astype(BF16)
    q2 = jnp.where(lane >= QK_DIM, q, 0.0).astype(BF16)
    for variant in (BEFORE, AFTER, DIAG):
        wq_ref[variant, 0:t, 0:LANES] = q1
        wq_ref[variant, t:2 * t, 0:LANES] = q2
    fast_ok = flag_ref[0] == 1

    def place(r):
        j = qi + r
        after = j < nk
        return (jnp.where(after, j, j - nk), jnp.where(after, AFTER, BEFORE),
                jnp.where(after, r, nk - r))

    def block(r, rows=(0, t)):
        row0, n = rows
        jw, variant, dist = (qi, DIAG, 0) if isinstance(r, int) and r == 0 else place(r)
        kt = kf_ref[pl.ds(pl.multiple_of(jw * t + row0, LANES), n), :]
        s = lax.dot_general(kt, wq_ref[variant], (((1,), (1,)), ((), ())),
                            preferred_element_type=F32)
        vt = vt_ref[jw, :, row0:row0 + n]
        if isinstance(r, int) and r == 0:
            bias = tdiag_ref[row0:row0 + n, :]
            return jnp.concatenate([s[:, 0:t] + bias, s[:, t:2 * t] + bias], axis=1), 0.0, vt
        return s, -c * ((dist - 1) * t).astype(F32), vt


    def fixed_reference(near_rows):
        if near_rows is None:
            visits = [(r, (0, t)) for r in range(nk)]
        else:
            visits = [(0, (0, t)), (1, (0, near_rows)), (nk - 1, (t - near_rows, near_rows))]
        m0 = mref_ref[...]
        for i, (r, rows) in enumerate(visits):
            s, shift, vt = block(r, rows)
            pv = jnp.dot(vt, jnp.exp2(s - (m0 - shift)).astype(BF16), preferred_element_type=F32)
            if i == 0:
                acc_ref[...] = pv
            else:
                acc_ref[...] += pv

    for near_rows in sorted(set(head_near_rows), key=lambda v: (v is None, v)):
        in_class = functools.reduce(jnp.logical_or,
                                    [h == hh for hh, v in enumerate(head_near_rows) if v == near_rows])
        pl.when(fast_ok & in_class)(functools.partial(fixed_reference, near_rows))

    @pl.when(jnp.logical_not(fast_ok))
    def _running_max():
        s, _, vt = block(0)
        m_ref[...] = jnp.max(s, axis=0, keepdims=True)
        acc_ref[...] = jnp.dot(vt, jnp.exp2(s - m_ref[...]).astype(BF16), preferred_element_type=F32)

        def fold(r, carry):
            s, shift, vt = block(r)
            m_old = m_ref[...]
            m_new = jnp.maximum(m_old, jnp.max(s, axis=0, keepdims=True) + shift)
            pv = jnp.dot(vt, jnp.exp2(s - (m_new - shift)).astype(BF16), preferred_element_type=F32)
            acc_ref[...] = jnp.exp2(m_old - m_new) * acc_ref[...] + pv
            m_ref[...] = m_new
            return carry

        lax.fori_loop(1, nk, fold, 0)

    lam = (jnp.exp(jnp.sum(lq1_ref[...] * lk1_ref[...], axis=-1, keepdims=True))
           - jnp.exp(jnp.sum(lq2_ref[...] * lk2_ref[...], axis=-1, keepdims=True))
           + lambda_init)
    rl = 1.0 / acc_ref[HEAD_DIM:HEAD_DIM + 1, :]
    o = (acc_ref[0:HEAD_DIM, 0:t] * rl[:, 0:t]
         - acc_ref[0:HEAD_DIM, t:2 * t] * (lam * rl[:, t:2 * t]))
    ms = jnp.mean(o * o, axis=0, keepdims=True)
    o = o * lax.rsqrt(ms + LN_EPS)
    o = o.T * gain_ref[...] * (1.0 - lambda_init)
    o_ref[0] = o.astype(o_ref.dtype)


def _attention(qkv, c, lq1, lk1, lq2, lk2, gain, *, batch, seq, lambda_init, t=1024):
    nh = N_HEADS
    vec = lambda n: pl.BlockSpec((1, n), lambda h, b, i: (0, 0))
    near = tuple(_near_rows(LOG2E * 2.0 ** (-8.0 * (i + 1) / nh), t, seq // t) for i in range(nh))
    return pl.pallas_call(
        functools.partial(_attn_kernel, t=t, lambda_init=lambda_init, head_near_rows=near),
        grid=(nh, batch, seq // t),
        in_specs=[pl.BlockSpec(memory_space=pltpu.SMEM),
                  vec(QK_DIM), vec(QK_DIM), vec(QK_DIM), vec(QK_DIM), vec(HEAD_DIM),
                  pl.BlockSpec((1, FEAT_ROWS, LANES), lambda h, b, i: (h, 0, 0)),
                  pl.BlockSpec((1, seq, HEAD_DIM), lambda h, b, i: (b, 0, h)),
                  pl.BlockSpec((1, seq, HEAD_DIM), lambda h, b, i: (b, 0, nh + h)),
                  pl.BlockSpec((1, seq, HEAD_DIM), lambda h, b, i: (b, 0, 2 * nh + h))],
        out_specs=pl.BlockSpec((1, t, HEAD_DIM), lambda h, b, i: (b, i, h)),
        out_shape=jax.ShapeDtypeStruct((batch, seq, ATTN_WIDTH), BF16),
        scratch_shapes=[pltpu.VMEM((seq, 2 * LANES), BF16),
                        pltpu.VMEM((seq // t, HEAD_DIM + ONES_ROWS, t), BF16),
                        pltpu.VMEM((3, 2 * t, 2 * LANES), BF16),
                        pltpu.VMEM((t, t), F32),
                        pltpu.VMEM((t, t), F32),
                        pltpu.VMEM((1, 2 * t), F32),
                        pltpu.VMEM((1, 2 * t), F32),
                        pltpu.VMEM((HEAD_DIM + ONES_ROWS, 2 * t), F32),
                        pltpu.SMEM((1,), jnp.int32)],
        compiler_params=_params("arbitrary", "arbitrary", "arbitrary"),
    )(c, lq1, lk1, lq2, lk2, gain, _feature_table(c, t), qkv, qkv, qkv)


def _merge_kernel(a_ref, s_ref, x_ref, wa_ref, ws_ref, wg0_ref, wg1_ref, bg0_ref, bg1_ref, o_ref):
    x = x_ref[...]
    g0 = jax.nn.sigmoid(jnp.dot(x, wg0_ref[...], preferred_element_type=F32) + bg0_ref[...])
    g1 = jax.nn.sigmoid(jnp.dot(x, wg1_ref[...], preferred_element_type=F32) + bg1_ref[...])
    ya = jnp.dot(a_ref[...], wa_ref[...], preferred_element_type=F32)
    ys = jnp.dot(s_ref[...], ws_ref[...], preferred_element_type=F32)
    o_ref[...] = (g0 * ya + g1 * ys).astype(o_ref.dtype)


def _merge(attn, sgo, x, w_in, gate_off, b_gate, w_ba, w_bs, *, tm=1024, tn=512):
    m, d = x.shape
    nb = d // tn
    j0 = gate_off // tn
    return pl.pallas_call(
        _merge_kernel,
        grid=(m // tm, d // tn),
        in_specs=[pl.BlockSpec((tm, attn.shape[1]), lambda i, j: (i, 0)),
                  pl.BlockSpec((tm, sgo.shape[1]), lambda i, j: (i, 0)),
                  pl.BlockSpec((tm, d), lambda i, j: (i, 0)),
                  pl.BlockSpec((w_ba.shape[0], tn), lambda i, j: (0, j)),
                  pl.BlockSpec((w_bs.shape[0], tn), lambda i, j: (0, j)),
                  pl.BlockSpec((d, tn), lambda i, j: (0, j0 + j)),
                  pl.BlockSpec((d, tn), lambda i, j: (0, j0 + nb + j)),
                  pl.BlockSpec((1, tn), lambda i, j: (0, j)),
                  pl.BlockSpec((1, tn), lambda i, j: (0, nb + j))],
        out_specs=pl.BlockSpec((tm, tn), lambda i, j: (i, j)),
        out_shape=jax.ShapeDtypeStruct((m, d), BF16),
        compiler_params=_params("parallel", "arbitrary"),
    )(attn, sgo, x, w_ba, w_bs, w_in, w_in, b_gate, b_gate)


def _oproj_kernel(y_ref, w_ref, x_ref, g_ref, b_ref, o_ref, *, alpha):
    for r in range(y_ref.shape[0] // OPROJ_PIECE_ROWS):
        rows = slice(r * OPROJ_PIECE_ROWS, (r + 1) * OPROJ_PIECE_ROWS)
        y = jnp.dot(y_ref[rows, :], w_ref[...], preferred_element_type=F32)
        o_ref[rows, :] = _layer_norm(alpha * x_ref[rows, :] + y, g_ref[...], b_ref[...])


OPROJ_PIECE_ROWS = 256


def _oproj(merged, w_o, x, g, b, *, alpha, tm=1024):
    m, d = x.shape
    return pl.pallas_call(
        functools.partial(_oproj_kernel, alpha=alpha),
        grid=(m // tm,),
        in_specs=[pl.BlockSpec((tm, d), lambda i: (i, 0)),
                  _resident((d, d)),
                  pl.BlockSpec((tm, d), lambda i: (i, 0)),
                  pl.BlockSpec((1, d), lambda i: (0, 0)),
                  pl.BlockSpec((1, d), lambda i: (0, 0))],
        out_specs=pl.BlockSpec((tm, d), lambda i: (i, 0)),
        out_shape=jax.ShapeDtypeStruct((m, d), F32),
        compiler_params=_params("parallel"),
    )(merged, w_o, x, g, b)


def _mlp_kernel(x_ref, wu_ref, wd_ref, g_ref, b_ref, o_ref, xb_ref, *, alpha):
    f = pl.program_id(1)
    nf = pl.num_programs(1)

    def step(first, last):
        for r in range(x_ref.shape[0] // MLP_PIECE_ROWS):
            rows = slice(r * MLP_PIECE_ROWS, (r + 1) * MLP_PIECE_ROWS)
            if first:
                xb_ref[rows, :] = x_ref[rows, :].astype(BF16)
            hid = jnp.dot(xb_ref[rows, :], wu_ref[...], preferred_element_type=F32)
            hid = jnp.square(jnp.maximum(hid, 0.0)).astype(BF16)
            y = jnp.dot(hid, wd_ref[...], preferred_element_type=F32)
            if not first:
                y = o_ref[rows, :] + y
            if last:
                y = _layer_norm(alpha * x_ref[rows, :] + y, g_ref[...], b_ref[...])
            o_ref[rows, :] = y

    pl.when(f == 0)(functools.partial(step, True, False))
    pl.when((f > 0) & (f < nf - 1))(functools.partial(step, False, False))
    pl.when(f == nf - 1)(functools.partial(step, False, True))


MLP_PIECE_ROWS = 256


def _mlp(x, w_up, w_down, g, b, *, alpha, tm=512, tf=1024):
    m, d = x.shape
    dff = w_up.shape[1]
    return pl.pallas_call(
        functools.partial(_mlp_kernel, alpha=alpha),
        grid=(m // tm, dff // tf),
        in_specs=[pl.BlockSpec((tm, d), lambda i, f: (i, 0)),
                  pl.BlockSpec((d, tf), lambda i, f: (0, f)),
                  pl.BlockSpec((tf, d), lambda i, f: (f, 0)),
                  pl.BlockSpec((1, d), lambda i, f: (0, 0)),
                  pl.BlockSpec((1, d), lambda i, f: (0, 0))],
        out_specs=pl.BlockSpec((tm, d), lambda i, f: (i, 0)),
        out_shape=jax.ShapeDtypeStruct((m, d), F32),
        scratch_shapes=[pltpu.VMEM((tm, d), BF16)],
        compiler_params=_params("parallel", "arbitrary"),
    )(x, w_up, w_down, g, b)


def kernel(x, w_in, lambda_q1, lambda_k1, lambda_q2, lambda_k2, attn_subln_g, sg_ln_g, sg_ln_b, sg_w_s, sg_b_s, b_gate, w_branch_attn, w_branch_sg, w_o, ln1_g, ln1_b, w_up, w_down, ln2_g, ln2_b):
    batch, seq, d = x.shape
    depth = w_in.shape[0]
    m = batch * seq
    alpha = (2.0 * depth) ** 0.25
    k_off = ATTN_WIDTH
    u_off = 3 * ATTN_WIDTH
    gate_off = u_off + 2 * SG_WIDTH
    row = lambda p: p.reshape(1, -1).astype(F32)

    i = jnp.arange(1, N_HEADS + 1, dtype=F32)
    slopes = jnp.exp2(-8.0 * i / N_HEADS)
    col_scale = jnp.where(jnp.arange(w_in.shape[2]) < k_off, LOG2E * QK_DIM ** -0.5, 1.0).astype(F32)

    xf = x.reshape(m, d)
    for l in range(depth):
        lambda_init = 0.8 - 0.6 * math.exp(-0.3 * l)
        w_in_b = (w_in[l] * col_scale).astype(BF16)
        qkv, xb = _proj_cast(xf, w_in_b, u_off)
        gd = SG_WIDTH // SG_GROUPS
        bs_full = jnp.repeat(sg_b_s[l].T.astype(F32), gd, axis=1)
        sgo = _sg_branch(xb, w_in_b, u_off, row(sg_ln_g[l]), row(sg_ln_b[l]), sg_w_s[l].astype(BF16), bs_full)

        attn = _attention(qkv.reshape(batch, seq, 3 * ATTN_WIDTH), LOG2E * slopes,
                          row(lambda_q1[l]), row(lambda_k1[l]), row(lambda_q2[l]), row(lambda_k2[l]),
                          row(attn_subln_g[l]), batch=batch, seq=seq, lambda_init=lambda_init)

        merged = _merge(attn.reshape(m, ATTN_WIDTH), sgo, xb, w_in_b, gate_off, row(b_gate[l]),
                        w_branch_attn[l].astype(BF16), w_branch_sg[l].astype(BF16))
        xf = _oproj(merged, w_o[l].astype(BF16), xf, row(ln1_g[l]), row(ln1_b[l]), alpha=alpha)
        xf = _mlp(xf, w_up[l].astype(BF16), w_down[l].astype(BF16), row(ln2_g[l]), row(ln2_b[l]), alpha=alpha)
    return xf.reshape(batch, seq, d)
```

```python
import functools
import math

import jax
import jax.numpy as jnp
from jax import lax
from jax.experimental import pallas as pl
from jax.experimental.pallas import tpu as pltpu

F32 = jnp.float32
BF16 = jnp.bfloat16

N_HEADS = 8
QK_DIM = 64
HEAD_DIM = 2 * QK_DIM
ATTN_WIDTH = N_HEADS * HEAD_DIM
SG_WIDTH = 1024
SG_GROUPS = 8
SG_CHUNK = 128
LN_EPS = 1e-5
INV_SQRT2 = 0.7071067811865476
LOG2E = 1.4426950408889634
LANES = 128

VMEM_LIMIT_BYTES = 60 * 1024 * 1024


def _params(*sem):
    return pltpu.CompilerParams(dimension_semantics=sem, vmem_limit_bytes=VMEM_LIMIT_BYTES)


def _layer_norm(y, g, b):
    mu = jnp.mean(y, axis=-1, keepdims=True)
    d = y - mu
    var = jnp.mean(d * d, axis=-1, keepdims=True)
    return d * lax.rsqrt(var + LN_EPS) * g + b


def _proj_cast_kernel(x_ref, w_ref, o_ref, xb_ref):
    xb = x_ref[...].astype(BF16)
    xb_ref[...] = xb
    o_ref[...] = jnp.dot(xb, w_ref[...], preferred_element_type=F32).astype(o_ref.dtype)


def _resident(shape, index=None):
    index = (0,) * len(shape) if index is None else index
    return pl.BlockSpec(shape, lambda i: index, pipeline_mode=pl.Buffered(1))


def _proj_cast(x, w, n, *, tm=512):
    m, k = x.shape
    return pl.pallas_call(
        _proj_cast_kernel,
        grid=(m // tm,),
        in_specs=[pl.BlockSpec((tm, k), lambda i: (i, 0)), _resident((k, n))],
        out_specs=[pl.BlockSpec((tm, n), lambda i: (i, 0)),
                   pl.BlockSpec((tm, k), lambda i: (i, 0))],
        out_shape=[jax.ShapeDtypeStruct((m, n), BF16),
                   jax.ShapeDtypeStruct((m, k), BF16)],
        compiler_params=_params("parallel"),
    )(x, w)


SG_PIECE_ROWS = 256


def _sg_branch_kernel(x_ref, wu_ref, wv_ref, g_ref, b_ref, ws_ref, bs_ref, o_ref, u_ref, vn_ref):
    gelu = lambda y: 0.5 * y * (1.0 + lax.erf(y * INV_SQRT2))
    pieces = x_ref.shape[0] // SG_PIECE_ROWS
    chunks = SG_PIECE_ROWS // SG_CHUNK
    gd = SG_WIDTH // SG_GROUPS
    for piece in range(pieces):
        rows = slice(piece * SG_PIECE_ROWS, (piece + 1) * SG_PIECE_ROWS)
        u_ref[rows, :] = gelu(jnp.dot(x_ref[rows, :], wu_ref[...], preferred_element_type=F32))
        v = gelu(jnp.dot(x_ref[rows, :], wv_ref[...], preferred_element_type=F32))
        vn_ref[rows, :] = _layer_norm(v, g_ref[...], b_ref[...]).astype(BF16)
    for piece in range(pieces):
        r0 = piece * SG_PIECE_ROWS
        for g in range(SG_GROUPS):
            cols = slice(g * gd, (g + 1) * gd)
            vg = jnp.concatenate([vn_ref[r0 + c * SG_CHUNK:r0 + (c + 1) * SG_CHUNK, cols]
                                  for c in range(chunks)], axis=1)
            mixed = jnp.dot(ws_ref[g], vg, preferred_element_type=F32)
            for c in range(chunks):
                mx = mixed[:, c * gd:(c + 1) * gd] + bs_ref[:, cols]
                rows = slice(r0 + c * SG_CHUNK, r0 + (c + 1) * SG_CHUNK)
                o_ref[rows, cols] = (u_ref[rows, cols] * mx).astype(o_ref.dtype)


def _sg_branch(x, w_in, u_off, ln_g, ln_b, w_s, bs_full, *, tm=512):
    m, k = x.shape
    ju = u_off // SG_WIDTH
    return pl.pallas_call(
        _sg_branch_kernel,
        grid=(m // tm,),
        in_specs=[pl.BlockSpec((tm, k), lambda i: (i, 0)),
                  _resident((k, SG_WIDTH), (0, ju)),
                  _resident((k, SG_WIDTH), (0, ju + 1)),
                  pl.BlockSpec((1, SG_WIDTH), lambda i: (0, 0)),
                  pl.BlockSpec((1, SG_WIDTH), lambda i: (0, 0)),
                  pl.BlockSpec(w_s.shape, lambda i: (0, 0, 0)),
                  pl.BlockSpec(bs_full.shape, lambda i: (0, 0))],
        out_specs=pl.BlockSpec((tm, SG_WIDTH), lambda i: (i, 0)),
        out_shape=jax.ShapeDtypeStruct((m, SG_WIDTH), BF16),
        scratch_shapes=[pltpu.VMEM((tm, SG_WIDTH), F32), pltpu.VMEM((tm, SG_WIDTH), BF16)],
        compiler_params=_params("parallel"),
    )(x, w_in, w_in, ln_g, ln_b, w_s, bs_full)


FEAT_ROWS = 16
ONES_ROWS = 16
BEFORE, AFTER, DIAG = 0, 1, 2
NORM_SLACK = 1.02
BOUND_LIMIT = 64.0
HEADROOM = 29.0
ZERO_EXPONENT = 155.0


def _pos_lo_bits(t):
    return max(0, t.bit_length() - 9)


def _feature_table(c, t):
    radix = float(1 << _pos_lo_bits(t))
    c1 = c.astype(BF16).astype(F32)
    c2 = (c - c1).astype(BF16).astype(F32)
    c3 = (c - c1 - c2).astype(BF16).astype(F32)
    nh = c.shape[0]
    z = jnp.zeros((nh, LANES), F32)
    ind = lambda lo: jnp.zeros((nh, LANES), F32).at[:, lo:lo + 3].set(1.0)
    pieces = jnp.stack([radix * c1, radix * c2, radix * c3, c1, c2, c3], axis=1)
    at = lambda lo: z.at[:, lo:lo + 6].set(pieces)
    rows = [ind(0), ind(3), ind(6), ind(9), at(12),
            at(6), at(0), ind(12), ind(15)]
    return jnp.stack(rows + [z] * (FEAT_ROWS - len(rows)), axis=1)


def _near_rows(c, t, nk):
    rows = -(-int(math.ceil((HEADROOM + ZERO_EXPONENT) / c - 1.0)) // LANES) * LANES
    return rows if rows <= t and nk >= 3 else None


def _max_sq_norms(x_ref, groups=8):
    n = x_ref.shape[1] // groups
    g = jnp.square(x_ref[0, 0:n, :].astype(F32))
    for i in range(1, groups):
        g = jnp.maximum(g, jnp.square(x_ref[0, i * n:(i + 1) * n, :].astype(F32)))
    lane = lax.broadcasted_iota(jnp.int32, g.shape, 1)
    out = []
    for mask in (lane < QK_DIM, lane >= QK_DIM):
        n2 = jnp.sum(jnp.where(mask, g, 0.0), axis=1, keepdims=True)
        out.append(jnp.max(n2, axis=0, keepdims=True))
    return out


def _attn_kernel(c_ref, lq1_ref, lk1_ref, lq2_ref, lk2_ref, gain_ref, tab_ref,
                 q_ref, k_ref, v_ref, o_ref,
                 kf_ref, vt_ref, wq_ref, tbase_ref, tdiag_ref, mref_ref, m_ref, lam_ref, acc_ref, flag_ref,
                 *, t, lambda_init, head_near_rows):
    h = pl.program_id(0)
    bi = pl.program_id(1)
    qi = pl.program_id(2)
    nk = k_ref.shape[1] // t
    c = c_ref[h]

    @pl.when((h == 0) & (bi == 0) & (qi == 0))
    def _once():
        key = lax.broadcasted_iota(jnp.int32, (t, t), 0)
        qry = lax.broadcasted_iota(jnp.int32, (t, t), 1)
        tbase_ref[...] = jnp.abs(key - qry).astype(F32)
        lam = (jnp.exp(jnp.sum(lq1_ref[...] * lk1_ref[...], axis=-1, keepdims=True))
               - jnp.exp(jnp.sum(lq2_ref[...] * lk2_ref[...], axis=-1, keepdims=True))
               + lambda_init)
        lam_ref[...] = jnp.broadcast_to(lam, lam_ref.shape)

    @pl.when((bi == 0) & (qi == 0))
    def _per_head():
        tdiag_ref[...] = -c * tbase_ref[...]
        pos = lax.broadcasted_iota(jnp.int32, (t, LANES), 0)
        lo_bits = _pos_lo_bits(t)
        split = lambda x: ((x >> lo_bits).astype(F32), (x & ((1 << lo_bits) - 1)).astype(F32))
        hi, lo = split(pos)
        rhi, rlo = split(t - pos)
        tab = tab_ref[0]
        kfeat = (hi * tab[0:1] + lo * tab[1:2] + rhi * tab[2:3] + rlo * tab[3:4] + tab[4:5]).astype(BF16)
        q_before = -(tab[5:6] + hi * tab[7:8] + lo * tab[8:9])
        q_after = -(tab[6:7] + rhi * tab[7:8] + rlo * tab[8:9])
        for j in range(nk):
            kf_ref[j * t:(j + 1) * t, LANES:2 * LANES] = kfeat
            vt_ref[j, HEAD_DIM:, :] = jnp.ones((ONES_ROWS, t), BF16)
        for half in range(2):
            rows = slice(half * t, (half + 1) * t)
            wq_ref[BEFORE, rows, LANES:2 * LANES] = q_before.astype(BF16)
            wq_ref[AFTER, rows, LANES:2 * LANES] = q_after.astype(BF16)
            wq_ref[DIAG, rows, LANES:2 * LANES] = jnp.zeros((t, LANES), BF16)

    @pl.when(qi == 0)
    def _per_batch_and_head():
        kf_ref[:, 0:LANES] = k_ref[0]
        for j in range(nk):
            vt_ref[j, 0:HEAD_DIM, :] = v_ref[0, j * t:(j + 1) * t, :].astype(F32).T.astype(BF16)
        qn = _max_sq_norms(q_ref)
        kn = _max_sq_norms(k_ref)
        bounds = [jnp.sqrt(a * b) * NORM_SLACK for a, b in zip(qn, kn)]
        for half in range(2):
            mref_ref[:, half * t:(half + 1) * t] = jnp.broadcast_to(bounds[half] - HEADROOM, (1, t))
        flag_ref[0] = (jnp.max(jnp.maximum(bounds[0], bounds[1])) <= BOUND_LIMIT).astype(jnp.int32)

    q = q_ref[0, pl.ds(pl.multiple_of(qi * t, t), t), :].astype(F32)
    lane = lax.broadcasted_iota(jnp.int32, q.shape, 1)
    q1 = jnp.where(lane < QK_DIM, q, 0.0).astype(BF16)
    q2 = jnp.where(lane >= QK_DIM, q, 0.0).astype(BF16)
    for variant in (BEFORE, AFTER, DIAG):
        wq_ref[variant, 0:t, 0:LANES] = q1
        wq_ref[variant, t:2 * t, 0:LANES] = q2
    fast_ok = flag_ref[0] == 1

    def place(r):
        j = qi + r
        after = j < nk
        return (jnp.where(after, j, j - nk), jnp.where(after, AFTER, BEFORE),
                jnp.where(after, r, nk - r))

    def block(r, rows=(0, t)):
        row0, n = rows
        jw, variant, dist = (qi, DIAG, 0) if isinstance(r, int) and r == 0 else place(r)
        kt = kf_ref[pl.ds(pl.multiple_of(jw * t + row0, LANES), n), :]
        s = lax.dot_general(kt, wq_ref[variant], (((1,), (1,)), ((), ())),
                            preferred_element_type=F32)
        vt = vt_ref[jw, :, row0:row0 + n]
        if isinstance(r, int) and r == 0:
            bias = tdiag_ref[row0:row0 + n, :]
            return jnp.concatenate([s[:, 0:t] + bias, s[:, t:2 * t] + bias], axis=1), 0.0, vt
        return s, -c * ((dist - 1) * t).astype(F32), vt


    def fixed_reference(near_rows):
        if near_rows is None:
            visits = [(r, (0, t)) for r in range(nk)]
        else:
            visits = [(0, (0, t)), (1, (0, near_rows)), (nk - 1, (t - near_rows, near_rows))]
        m0 = mref_ref[...]
        for i, (r, rows) in enumerate(visits):
            s, shift, vt = block(r, rows)
            pv = jnp.dot(vt, jnp.exp2(s - (m0 - shift)).astype(BF16), preferred_element_type=F32)
            if i == 0:
                acc_ref[...] = pv
            else:
                acc_ref[...] += pv

    for near_rows in sorted(set(head_near_rows), key=lambda v: (v is None, v)):
        in_class = functools.reduce(jnp.logical_or,
                                    [h == hh for hh, v in enumerate(head_near_rows) if v == near_rows])
        pl.when(fast_ok & in_class)(functools.partial(fixed_reference, near_rows))

    @pl.when(jnp.logical_not(fast_ok))
    def _running_max():
        s, _, vt = block(0)
        m_ref[...] = jnp.max(s, axis=0, keepdims=True)
        acc_ref[...] = jnp.dot(vt, jnp.exp2(s - m_ref[...]).astype(BF16), preferred_element_type=F32)

        def fold(r, carry):
            s, shift, vt = block(r)
            m_old = m_ref[...]
            m_new = jnp.maximum(m_old, jnp.max(s, axis=0, keepdims=True) + shift)
            pv = jnp.dot(vt, jnp.exp2(s - (m_new - shift)).astype(BF16), preferred_element_type=F32)
            acc_ref[...] = jnp.exp2(m_old - m_new) * acc_ref[...] + pv
            m_ref[...] = m_new
            return carry

        lax.fori_loop(1, nk, fold, 0)

    rl = 1.0 / acc_ref[HEAD_DIM:HEAD_DIM + 1, :]
    o = (acc_ref[0:HEAD_DIM, 0:t] * rl[:, 0:t]
         - acc_ref[0:HEAD_DIM, t:2 * t] * (lam_ref[:, 0:1] * rl[:, t:2 * t]))
    ms = jnp.mean(o * o, axis=0, keepdims=True)
    o = o * lax.rsqrt(ms + LN_EPS)
    o = o.T * gain_ref[...] * (1.0 - lambda_init)
    o_ref[0] = o.astype(o_ref.dtype)


def _attention(qkv, c, lq1, lk1, lq2, lk2, gain, *, batch, seq, lambda_init, t=1024):
    nh = N_HEADS
    vec = lambda n: pl.BlockSpec((1, n), lambda h, b, i: (0, 0))
    near = tuple(_near_rows(LOG2E * 2.0 ** (-8.0 * (i + 1) / nh), t, seq // t) for i in range(nh))
    return pl.pallas_call(
        functools.partial(_attn_kernel, t=t, lambda_init=lambda_init, head_near_rows=near),
        grid=(nh, batch, seq // t),
        in_specs=[pl.BlockSpec(memory_space=pltpu.SMEM),
                  vec(QK_DIM), vec(QK_DIM), vec(QK_DIM), vec(QK_DIM), vec(HEAD_DIM),
                  pl.BlockSpec((1, FEAT_ROWS, LANES), lambda h, b, i: (h, 0, 0)),
                  pl.BlockSpec((1, seq, HEAD_DIM), lambda h, b, i: (b, 0, h)),
                  pl.BlockSpec((1, seq, HEAD_DIM), lambda h, b, i: (b, 0, nh + h)),
                  pl.BlockSpec((1, seq, HEAD_DIM), lambda h, b, i: (b, 0, 2 * nh + h))],
        out_specs=pl.BlockSpec((1, t, HEAD_DIM), lambda h, b, i: (b, i, h)),
        out_shape=jax.ShapeDtypeStruct((batch, seq, ATTN_WIDTH), BF16),
        scratch_shapes=[pltpu.VMEM((seq, 2 * LANES), BF16),
                        pltpu.VMEM((seq // t, HEAD_DIM + ONES_ROWS, t), BF16),
                        pltpu.VMEM((3, 2 * t, 2 * LANES), BF16),
                        pltpu.VMEM((t, t), F32),
                        pltpu.VMEM((t, t), F32),
                        pltpu.VMEM((1, 2 * t), F32),
                        pltpu.VMEM((1, 2 * t), F32),
                        pltpu.VMEM((1, LANES), F32),
                        pltpu.VMEM((HEAD_DIM + ONES_ROWS, 2 * t), F32),
                        pltpu.SMEM((1,), jnp.int32)],
        compiler_params=_params("arbitrary", "arbitrary", "arbitrary"),
    )(c, lq1, lk1, lq2, lk2, gain, _feature_table(c, t), qkv, qkv, qkv)


def _merge_kernel(a_ref, s_ref, x_ref, wa_ref, ws_ref, wg0_ref, wg1_ref, bg0_ref, bg1_ref, o_ref):
    x = x_ref[...]
    g0 = jax.nn.sigmoid(jnp.dot(x, wg0_ref[...], preferred_element_type=F32) + bg0_ref[...])
    g1 = jax.nn.sigmoid(jnp.dot(x, wg1_ref[...], preferred_element_type=F32) + bg1_ref[...])
    ya = jnp.dot(a_ref[...], wa_ref[...], preferred_element_type=F32)
    ys = jnp.dot(s_ref[...], ws_ref[...], preferred_element_type=F32)
    o_ref[...] = (g0 * ya + g1 * ys).astype(o_ref.dtype)


def _merge(attn, sgo, x, w_in, gate_off, b_gate, w_ba, w_bs, *, tm=1024, tn=512):
    m, d = x.shape
    nb = d // tn
    j0 = gate_off // tn
    return pl.pallas_call(
        _merge_kernel,
        grid=(m // tm, d // tn),
        in_specs=[pl.BlockSpec((tm, attn.shape[1]), lambda i, j: (i, 0)),
                  pl.BlockSpec((tm, sgo.shape[1]), lambda i, j: (i, 0)),
                  pl.BlockSpec((tm, d), lambda i, j: (i, 0)),
                  pl.BlockSpec((w_ba.shape[0], tn), lambda i, j: (0, j)),
                  pl.BlockSpec((w_bs.shape[0], tn), lambda i, j: (0, j)),
                  pl.BlockSpec((d, tn), lambda i, j: (0, j0 + j)),
                  pl.BlockSpec((d, tn), lambda i, j: (0, j0 + nb + j)),
                  pl.BlockSpec((1, tn), lambda i, j: (0, j)),
                  pl.BlockSpec((1, tn), lambda i, j: (0, nb + j))],
        out_specs=pl.BlockSpec((tm, tn), lambda i, j: (i, j)),
        out_shape=jax.ShapeDtypeStruct((m, d), BF16),
        compiler_params=_params("parallel", "arbitrary"),
    )(attn, sgo, x, w_ba, w_bs, w_in, w_in, b_gate, b_gate)


def _oproj_kernel(y_ref, w_ref, x_ref, g_ref, b_ref, o_ref, *, alpha):
    for r in range(y_ref.shape[0] // OPROJ_PIECE_ROWS):
        rows = slice(r * OPROJ_PIECE_ROWS, (r + 1) * OPROJ_PIECE_ROWS)
        y = jnp.dot(y_ref[rows, :], w_ref[...], preferred_element_type=F32)
        o_ref[rows, :] = _layer_norm(alpha * x_ref[rows, :] + y, g_ref[...], b_ref[...])


OPROJ_PIECE_ROWS = 256


def _oproj(merged, w_o, x, g, b, *, alpha, tm=1024):
    m, d = x.shape
    return pl.pallas_call(
        functools.partial(_oproj_kernel, alpha=alpha),
        grid=(m // tm,),
        in_specs=[pl.BlockSpec((tm, d), lambda i: (i, 0)),
                  _resident((d, d)),
                  pl.BlockSpec((tm, d), lambda i: (i, 0)),
                  pl.BlockSpec((1, d), lambda i: (0, 0)),
                  pl.BlockSpec((1, d), lambda i: (0, 0))],
        out_specs=pl.BlockSpec((tm, d), lambda i: (i, 0)),
        out_shape=jax.ShapeDtypeStruct((m, d), F32),
        compiler_params=_params("parallel"),
    )(merged, w_o, x, g, b)


def _mlp_kernel(x_ref, wu_ref, wd_ref, g_ref, b_ref, o_ref, xb_ref, *, alpha):
    f = pl.program_id(1)
    nf = pl.num_programs(1)

    def step(first, last):
        for r in range(x_ref.shape[0] // MLP_PIECE_ROWS):
            rows = slice(r * MLP_PIECE_ROWS, (r + 1) * MLP_PIECE_ROWS)
            if first:
                xb_ref[rows, :] = x_ref[rows, :].astype(BF16)
            hid = jnp.dot(xb_ref[rows, :], wu_ref[...], preferred_element_type=F32)
            hid = jnp.square(jnp.maximum(hid, 0.0)).astype(BF16)
            y = jnp.dot(hid, wd_ref[...], preferred_element_type=F32)
            if not first:
                y = o_ref[rows, :] + y
            if last:
                y = _layer_norm(alpha * x_ref[rows, :] + y, g_ref[...], b_ref[...])
            o_ref[rows, :] = y

    pl.when(f == 0)(functools.partial(step, True, False))
    pl.when((f > 0) & (f < nf - 1))(functools.partial(step, False, False))
    pl.when(f == nf - 1)(functools.partial(step, False, True))


MLP_PIECE_ROWS = 256


def _mlp(x, w_up, w_down, g, b, *, alpha, tm=512, tf=1024):
    m, d = x.shape
    dff = w_up.shape[1]
    return pl.pallas_call(
        functools.partial(_mlp_kernel, alpha=alpha),
        grid=(m // tm, dff // tf),
        in_specs=[pl.BlockSpec((tm, d), lambda i, f: (i, 0)),
                  pl.BlockSpec((d, tf), lambda i, f: (0, f)),
                  pl.BlockSpec((tf, d), lambda i, f: (f, 0)),
                  pl.BlockSpec((1, d), lambda i, f: (0, 0)),
                  pl.BlockSpec((1, d), lambda i, f: (0, 0))],
        out_specs=pl.BlockSpec((tm, d), lambda i, f: (i, 0)),
        out_shape=jax.ShapeDtypeStruct((m, d), F32),
        scratch_shapes=[pltpu.VMEM((tm, d), BF16)],
        compiler_params=_params("parallel", "arbitrary"),
    )(x, w_up, w_down, g, b)


def kernel(x, w_in, lambda_q1, lambda_k1, lambda_q2, lambda_k2, attn_subln_g, sg_ln_g, sg_ln_b, sg_w_s, sg_b_s, b_gate, w_branch_attn, w_branch_sg, w_o, ln1_g, ln1_b, w_up, w_down, ln2_g, ln2_b):
    batch, seq, d = x.shape
    depth = w_in.shape[0]
    m = batch * seq
    alpha = (2.0 * depth) ** 0.25
    k_off = ATTN_WIDTH
    u_off = 3 * ATTN_WIDTH
    gate_off = u_off + 2 * SG_WIDTH
    row = lambda p: p.reshape(1, -1).astype(F32)

    i = jnp.arange(1, N_HEADS + 1, dtype=F32)
    slopes = jnp.exp2(-8.0 * i / N_HEADS)
    col_scale = jnp.where(jnp.arange(w_in.shape[2]) < k_off, LOG2E * QK_DIM ** -0.5, 1.0).astype(F32)

    xf = x.reshape(m, d)
    for l in range(depth):
        lambda_init = 0.8 - 0.6 * math.exp(-0.3 * l)
        w_in_b = (w_in[l] * col_scale).astype(BF16)
        qkv, xb = _proj_cast(xf, w_in_b, u_off)
        gd = SG_WIDTH // SG_GROUPS
        bs_full = jnp.repeat(sg_b_s[l].T.astype(F32), gd, axis=1)
        sgo = _sg_branch(xb, w_in_b, u_off, row(sg_ln_g[l]), row(sg_ln_b[l]), sg_w_s[l].astype(BF16), bs_full)

        attn = _attention(qkv.reshape(batch, seq, 3 * ATTN_WIDTH), LOG2E * slopes,
                          row(lambda_q1[l]), row(lambda_k1[l]), row(lambda_q2[l]), row(lambda_k2[l]),
                          row(attn_subln_g[l]), batch=batch, seq=seq, lambda_init=lambda_init)

        merged = _merge(attn.reshape(m, ATTN_WIDTH), sgo, xb, w_in_b, gate_off, row(b_gate[l]),
                        w_branch_attn[l].astype(BF16), w_branch_sg[l].astype(BF16))
        xf = _oproj(merged, w_o[l].astype(BF16), xf, row(ln1_g[l]), row(ln1_b[l]), alpha=alpha)
        xf = _mlp(xf, w_up[l].astype(BF16), w_down[l].astype(BF16), row(ln2_g[l]), row(ln2_b[l]), alpha=alpha)
    return xf.reshape(batch, seq, d)
```

```python
import functools
import math

import jax
import jax.numpy as jnp
from jax import lax
from jax.experimental import pallas as pl
from jax.experimental.pallas import tpu as pltpu

F32 = jnp.float32
BF16 = jnp.bfloat16

N_HEADS = 8
QK_DIM = 64
HEAD_DIM = 2 * QK_DIM
ATTN_WIDTH = N_HEADS * HEAD_DIM
SG_WIDTH = 1024
SG_GROUPS = 8
SG_CHUNK = 128
LN_EPS = 1e-5
INV_SQRT2 = 0.7071067811865476
LOG2E = 1.4426950408889634
LANES = 128

VMEM_LIMIT_BYTES = 60 * 1024 * 1024


def _params(*sem):
    return pltpu.CompilerParams(dimension_semantics=sem, vmem_limit_bytes=VMEM_LIMIT_BYTES)


def _layer_norm(y, g, b):
    mu = jnp.mean(y, axis=-1, keepdims=True)
    d = y - mu
    var = jnp.mean(d * d, axis=-1, keepdims=True)
    return d * lax.rsqrt(var + LN_EPS) * g + b


def _proj_cast_kernel(x_ref, w_ref, o_ref, xb_ref):
    xb = x_ref[...].astype(BF16)
    xb_ref[...] = xb
    o_ref[...] = jnp.dot(xb, w_ref[...], preferred_element_type=F32).astype(o_ref.dtype)


def _resident(shape, index=None):
    index = (0,) * len(shape) if index is None else index
    return pl.BlockSpec(shape, lambda i: index, pipeline_mode=pl.Buffered(1))


def _proj_cast(x, w, n, *, tm=512):
    m, k = x.shape
    return pl.pallas_call(
        _proj_cast_kernel,
        grid=(m // tm,),
        in_specs=[pl.BlockSpec((tm, k), lambda i: (i, 0)), _resident((k, n))],
        out_specs=[pl.BlockSpec((tm, n), lambda i: (i, 0)),
                   pl.BlockSpec((tm, k), lambda i: (i, 0))],
        out_shape=[jax.ShapeDtypeStruct((m, n), BF16),
                   jax.ShapeDtypeStruct((m, k), BF16)],
        compiler_params=_params("parallel"),
    )(x, w)


SG_PIECE_ROWS = 256


def _sg_branch_kernel(x_ref, wu_ref, wv_ref, g_ref, b_ref, ws_ref, bs_ref, o_ref, u_ref, vn_ref):
    gelu = lambda y: 0.5 * y * (1.0 + lax.erf(y * INV_SQRT2))
    pieces = x_ref.shape[0] // SG_PIECE_ROWS
    chunks = SG_PIECE_ROWS // SG_CHUNK
    gd = SG_WIDTH // SG_GROUPS
    for piece in range(pieces):
        rows = slice(piece * SG_PIECE_ROWS, (piece + 1) * SG_PIECE_ROWS)
        u_ref[rows, :] = gelu(jnp.dot(x_ref[rows, :], wu_ref[...], preferred_element_type=F32))
        v = gelu(jnp.dot(x_ref[rows, :], wv_ref[...], preferred_element_type=F32))
        vn_ref[rows, :] = _layer_norm(v, g_ref[...], b_ref[...]).astype(BF16)
    for piece in range(pieces):
        r0 = piece * SG_PIECE_ROWS
        for g in range(SG_GROUPS):
            cols = slice(g * gd, (g + 1) * gd)
            vg = jnp.concatenate([vn_ref[r0 + c * SG_CHUNK:r0 + (c + 1) * SG_CHUNK, cols]
                                  for c in range(chunks)], axis=1)
            mixed = jnp.dot(ws_ref[g], vg, preferred_element_type=F32)
            for c in range(chunks):
                mx = mixed[:, c * gd:(c + 1) * gd] + bs_ref[:, cols]
                rows = slice(r0 + c * SG_CHUNK, r0 + (c + 1) * SG_CHUNK)
                o_ref[rows, cols] = (u_ref[rows, cols] * mx).astype(o_ref.dtype)


def _sg_branch(x, w_in, u_off, ln_g, ln_b, w_s, bs_full, *, tm=512):
    m, k = x.shape
    ju = u_off // SG_WIDTH
    return pl.pallas_call(
        _sg_branch_kernel,
        grid=(m // tm,),
        in_specs=[pl.BlockSpec((tm, k), lambda i: (i, 0)),
                  _resident((k, SG_WIDTH), (0, ju)),
                  _resident((k, SG_WIDTH), (0, ju + 1)),
                  pl.BlockSpec((1, SG_WIDTH), lambda i: (0, 0)),
                  pl.BlockSpec((1, SG_WIDTH), lambda i: (0, 0)),
                  pl.BlockSpec(w_s.shape, lambda i: (0, 0, 0)),
                  pl.BlockSpec(bs_full.shape, lambda i: (0, 0))],
        out_specs=pl.BlockSpec((tm, SG_WIDTH), lambda i: (i, 0)),
        out_shape=jax.ShapeDtypeStruct((m, SG_WIDTH), BF16),
        scratch_shapes=[pltpu.VMEM((tm, SG_WIDTH), F32), pltpu.VMEM((tm, SG_WIDTH), BF16)],
        compiler_params=_params("parallel"),
    )(x, w_in, w_in, ln_g, ln_b, w_s, bs_full)


FEAT_ROWS = 16
ONES_ROWS = 16
BAND_GROUP = 256
BEFORE, AFTER, DIAG = 0, 1, 2
NORM_SLACK = 1.02
BOUND_LIMIT = 64.0
HEADROOM = 29.0
ZERO_EXPONENT = 155.0


def _pos_lo_bits(t):
    return max(0, t.bit_length() - 9)


def _feature_table(c, t):
    radix = float(1 << _pos_lo_bits(t))
    c1 = c.astype(BF16).astype(F32)
    c2 = (c - c1).astype(BF16).astype(F32)
    c3 = (c - c1 - c2).astype(BF16).astype(F32)
    nh = c.shape[0]
    z = jnp.zeros((nh, LANES), F32)
    ind = lambda lo: jnp.zeros((nh, LANES), F32).at[:, lo:lo + 3].set(1.0)
    pieces = jnp.stack([radix * c1, radix * c2, radix * c3, c1, c2, c3], axis=1)
    at = lambda lo: z.at[:, lo:lo + 6].set(pieces)
    rows = [ind(0), ind(3), ind(6), ind(9), at(12),
            at(6), at(0), ind(12), ind(15)]
    return jnp.stack(rows + [z] * (FEAT_ROWS - len(rows)), axis=1)


def _near_rows(c, t, nk):
    rows = -(-int(math.ceil((HEADROOM + ZERO_EXPONENT) / c - 1.0)) // LANES) * LANES
    return rows if rows <= t and nk >= 3 else None


def _max_sq_norms(x_ref, groups=8):
    n = x_ref.shape[1] // groups
    g = jnp.square(x_ref[0, 0:n, :].astype(F32))
    for i in range(1, groups):
        g = jnp.maximum(g, jnp.square(x_ref[0, i * n:(i + 1) * n, :].astype(F32)))
    lane = lax.broadcasted_iota(jnp.int32, g.shape, 1)
    out = []
    for mask in (lane < QK_DIM, lane >= QK_DIM):
        n2 = jnp.sum(jnp.where(mask, g, 0.0), axis=1, keepdims=True)
        out.append(jnp.max(n2, axis=0, keepdims=True))
    return out


def _attn_kernel(c_ref, lq1_ref, lk1_ref, lq2_ref, lk2_ref, gain_ref, tab_ref,
                 q_ref, k_ref, v_ref, o_ref,
                 kf_ref, vt_ref, wq_ref, tbase_ref, tdiag_ref, mref_ref, m_ref, lam_ref, acc_ref, flag_ref,
                 *, t, lambda_init, head_near_rows):
    h = pl.program_id(0)
    bi = pl.program_id(1)
    qi = pl.program_id(2)
    nk = k_ref.shape[1] // t
    c = c_ref[h]

    @pl.when((h == 0) & (bi == 0) & (qi == 0))
    def _once():
        key = lax.broadcasted_iota(jnp.int32, (t, t), 0)
        qry = lax.broadcasted_iota(jnp.int32, (t, t), 1)
        tbase_ref[...] = jnp.abs(key - qry).astype(F32)
        lam = (jnp.exp(jnp.sum(lq1_ref[...] * lk1_ref[...], axis=-1, keepdims=True))
               - jnp.exp(jnp.sum(lq2_ref[...] * lk2_ref[...], axis=-1, keepdims=True))
               + lambda_init)
        lam_ref[...] = jnp.broadcast_to(lam, lam_ref.shape)

    @pl.when((bi == 0) & (qi == 0))
    def _per_head():
        tdiag_ref[...] = -c * tbase_ref[...]
        pos = lax.broadcasted_iota(jnp.int32, (t, LANES), 0)
        lo_bits = _pos_lo_bits(t)
        split = lambda x: ((x >> lo_bits).astype(F32), (x & ((1 << lo_bits) - 1)).astype(F32))
        hi, lo = split(pos)
        rhi, rlo = split(t - pos)
        tab = tab_ref[0]
        kfeat = (hi * tab[0:1] + lo * tab[1:2] + rhi * tab[2:3] + rlo * tab[3:4] + tab[4:5]).astype(BF16)
        q_before = -(tab[5:6] + hi * tab[7:8] + lo * tab[8:9])
        q_after = -(tab[6:7] + rhi * tab[7:8] + rlo * tab[8:9])
        for j in range(nk):
            kf_ref[j * t:(j + 1) * t, LANES:2 * LANES] = kfeat
            vt_ref[j, HEAD_DIM:, :] = jnp.ones((ONES_ROWS, t), BF16)
        for half in range(2):
            rows = slice(half * t, (half + 1) * t)
            wq_ref[BEFORE, rows, LANES:2 * LANES] = q_before.astype(BF16)
            wq_ref[AFTER, rows, LANES:2 * LANES] = q_after.astype(BF16)
            wq_ref[DIAG, rows, LANES:2 * LANES] = jnp.zeros((t, LANES), BF16)

    @pl.when(qi == 0)
    def _per_batch_and_head():
        kf_ref[:, 0:LANES] = k_ref[0]
        for j in range(nk):
            vt_ref[j, 0:HEAD_DIM, :] = v_ref[0, j * t:(j + 1) * t, :].astype(F32).T.astype(BF16)
        qn = _max_sq_norms(q_ref)
        kn = _max_sq_norms(k_ref)
        bounds = [jnp.sqrt(a * b) * NORM_SLACK for a, b in zip(qn, kn)]
        for half in range(2):
            mref_ref[:, half * t:(half + 1) * t] = jnp.broadcast_to(bounds[half] - HEADROOM, (1, t))
        flag_ref[0] = (jnp.max(jnp.maximum(bounds[0], bounds[1])) <= BOUND_LIMIT).astype(jnp.int32)

    q = q_ref[0, pl.ds(pl.multiple_of(qi * t, t), t), :].astype(F32)
    lane = lax.broadcasted_iota(jnp.int32, q.shape, 1)
    q1 = jnp.where(lane < QK_DIM, q, 0.0).astype(BF16)
    q2 = jnp.where(lane >= QK_DIM, q, 0.0).astype(BF16)
    for variant in (BEFORE, AFTER, DIAG):
        wq_ref[variant, 0:t, 0:LANES] = q1
        wq_ref[variant, t:2 * t, 0:LANES] = q2
    fast_ok = flag_ref[0] == 1

    def place(r):
        j = qi + r
        after = j < nk
        return (jnp.where(after, j, j - nk), jnp.where(after, AFTER, BEFORE),
                jnp.where(after, r, nk - r))

    def block(r, rows=(0, t), qcols=(0, t)):
        row0, n = rows
        q0, nq = qcols
        jw, variant, dist = (qi, DIAG, 0) if isinstance(r, int) and r == 0 else place(r)
        kt = kf_ref[pl.ds(pl.multiple_of(jw * t + row0, LANES), n), :]
        if nq == t:
            wq = wq_ref[variant]
        else:
            wq = jnp.concatenate([wq_ref[variant, q0:q0 + nq, :], wq_ref[variant, t + q0:t + q0 + nq, :]], axis=0)
        s = lax.dot_general(kt, wq, (((1,), (1,)), ((), ())), preferred_element_type=F32)
        vt = vt_ref[jw, :, row0:row0 + n]
        if isinstance(r, int) and r == 0:
            bias = tdiag_ref[row0:row0 + n, q0:q0 + nq]
            return jnp.concatenate([s[:, 0:nq] + bias, s[:, nq:2 * nq] + bias], axis=1), 0.0, vt
        return s, -c * ((dist - 1) * t).astype(F32), vt


    def fixed_reference(near_rows):
        if near_rows is None:
            plan = [((0, t), [(r, (0, t)) for r in range(nk)])]
        else:
            plan = []
            for q0 in range(0, t, BAND_GROUP):
                q1 = q0 + BAND_GROUP
                lo, hi = max(0, q0 - near_rows), min(t, q1 + near_rows)
                visits = [(0, (lo, hi - lo))]
                n_after = min(t, near_rows - (t - q1))
                n_before = min(t, near_rows - q0)
                if n_after > 0:
                    visits.append((1, (0, n_after)))
                if n_before > 0:
                    visits.append((nk - 1, (t - n_before, n_before)))
                plan.append(((q0, BAND_GROUP), visits))
        for (q0, nq), visits in plan:
            m0 = jnp.concatenate([mref_ref[:, q0:q0 + nq], mref_ref[:, t + q0:t + q0 + nq]], axis=1)
            for i, (r, rows) in enumerate(visits):
                s, shift, vt = block(r, rows, (q0, nq))
                pv = jnp.dot(vt, jnp.exp2(s - (m0 - shift)).astype(BF16), preferred_element_type=F32)
                for half, dst in enumerate((q0, t + q0)):
                    part = pv[:, half * nq:(half + 1) * nq]
                    if i == 0:
                        acc_ref[:, dst:dst + nq] = part
                    else:
                        acc_ref[:, dst:dst + nq] += part

    for near_rows in sorted(set(head_near_rows), key=lambda v: (v is None, v)):
        in_class = functools.reduce(jnp.logical_or,
                                    [h == hh for hh, v in enumerate(head_near_rows) if v == near_rows])
        pl.when(fast_ok & in_class)(functools.partial(fixed_reference, near_rows))

    @pl.when(jnp.logical_not(fast_ok))
    def _running_max():
        s, _, vt = block(0)
        m_ref[...] = jnp.max(s, axis=0, keepdims=True)
        acc_ref[...] = jnp.dot(vt, jnp.exp2(s - m_ref[...]).astype(BF16), preferred_element_type=F32)

        def fold(r, carry):
            s, shift, vt = block(r)
            m_old = m_ref[...]
            m_new = jnp.maximum(m_old, jnp.max(s, axis=0, keepdims=True) + shift)
            pv = jnp.dot(vt, jnp.exp2(s - (m_new - shift)).astype(BF16), preferred_element_type=F32)
            acc_ref[...] = jnp.exp2(m_old - m_new) * acc_ref[...] + pv
            m_ref[...] = m_new
            return carry

        lax.fori_loop(1, nk, fold, 0)

    rl = 1.0 / acc_ref[HEAD_DIM:HEAD_DIM + 1, :]
    o = (acc_ref[0:HEAD_DIM, 0:t] * rl[:, 0:t]
         - acc_ref[0:HEAD_DIM, t:2 * t] * (lam_ref[:, 0:1] * rl[:, t:2 * t]))
    ms = jnp.mean(o * o, axis=0, keepdims=True)
    o = o * lax.rsqrt(ms + LN_EPS)
    o = o.T * gain_ref[...] * (1.0 - lambda_init)
    o_ref[0] = o.astype(o_ref.dtype)


def _attention(qkv, c, lq1, lk1, lq2, lk2, gain, *, batch, seq, lambda_init, t=1024):
    nh = N_HEADS
    vec = lambda n: pl.BlockSpec((1, n), lambda h, b, i: (0, 0))
    near = tuple(_near_rows(LOG2E * 2.0 ** (-8.0 * (i + 1) / nh), t, seq // t) for i in range(nh))
    return pl.pallas_call(
        functools.partial(_attn_kernel, t=t, lambda_init=lambda_init, head_near_rows=near),
        grid=(nh, batch, seq // t),
        in_specs=[pl.BlockSpec(memory_space=pltpu.SMEM),
                  vec(QK_DIM), vec(QK_DIM), vec(QK_DIM), vec(QK_DIM), vec(HEAD_DIM),
                  pl.BlockSpec((1, FEAT_ROWS, LANES), lambda h, b, i: (h, 0, 0)),
                  pl.BlockSpec((1, seq, HEAD_DIM), lambda h, b, i: (b, 0, h)),
                  pl.BlockSpec((1, seq, HEAD_DIM), lambda h, b, i: (b, 0, nh + h)),
                  pl.BlockSpec((1, seq, HEAD_DIM), lambda h, b, i: (b, 0, 2 * nh + h))],
        out_specs=pl.BlockSpec((1, t, HEAD_DIM), lambda h, b, i: (b, i, h)),
        out_shape=jax.ShapeDtypeStruct((batch, seq, ATTN_WIDTH), BF16),
        scratch_shapes=[pltpu.VMEM((seq, 2 * LANES), BF16),
                        pltpu.VMEM((seq // t, HEAD_DIM + ONES_ROWS, t), BF16),
                        pltpu.VMEM((3, 2 * t, 2 * LANES), BF16),
                        pltpu.VMEM((t, t), F32),
                        pltpu.VMEM((t, t), F32),
                        pltpu.VMEM((1, 2 * t), F32),
                        pltpu.VMEM((1, 2 * t), F32),
                        pltpu.VMEM((1, LANES), F32),
                        pltpu.VMEM((HEAD_DIM + ONES_ROWS, 2 * t), F32),
                        pltpu.SMEM((1,), jnp.int32)],
        compiler_params=_params("arbitrary", "arbitrary", "arbitrary"),
    )(c, lq1, lk1, lq2, lk2, gain, _feature_table(c, t), qkv, qkv, qkv)


def _merge_kernel(a_ref, s_ref, x_ref, wa_ref, ws_ref, wg0_ref, wg1_ref, bg0_ref, bg1_ref, o_ref):
    x = x_ref[...]
    g0 = jax.nn.sigmoid(jnp.dot(x, wg0_ref[...], preferred_element_type=F32) + bg0_ref[...])
    g1 = jax.nn.sigmoid(jnp.dot(x, wg1_ref[...], preferred_element_type=F32) + bg1_ref[...])
    ya = jnp.dot(a_ref[...], wa_ref[...], preferred_element_type=F32)
    ys = jnp.dot(s_ref[...], ws_ref[...], preferred_element_type=F32)
    o_ref[...] = (g0 * ya + g1 * ys).astype(o_ref.dtype)


def _merge(attn, sgo, x, w_in, gate_off, b_gate, w_ba, w_bs, *, tm=1024, tn=512):
    m, d = x.shape
    nb = d // tn
    j0 = gate_off // tn
    return pl.pallas_call(
        _merge_kernel,
        grid=(m // tm, d // tn),
        in_specs=[pl.BlockSpec((tm, attn.shape[1]), lambda i, j: (i, 0)),
                  pl.BlockSpec((tm, sgo.shape[1]), lambda i, j: (i, 0)),
                  pl.BlockSpec((tm, d), lambda i, j: (i, 0)),
                  pl.BlockSpec((w_ba.shape[0], tn), lambda i, j: (0, j)),
                  pl.BlockSpec((w_bs.shape[0], tn), lambda i, j: (0, j)),
                  pl.BlockSpec((d, tn), lambda i, j: (0, j0 + j)),
                  pl.BlockSpec((d, tn), lambda i, j: (0, j0 + nb + j)),
                  pl.BlockSpec((1, tn), lambda i, j: (0, j)),
                  pl.BlockSpec((1, tn), lambda i, j: (0, nb + j))],
        out_specs=pl.BlockSpec((tm, tn), lambda i, j: (i, j)),
        out_shape=jax.ShapeDtypeStruct((m, d), BF16),
        compiler_params=_params("parallel", "arbitrary"),
    )(attn, sgo, x, w_ba, w_bs, w_in, w_in, b_gate, b_gate)


def _oproj_kernel(y_ref, w_ref, x_ref, g_ref, b_ref, o_ref, *, alpha):
    for r in range(y_ref.shape[0] // OPROJ_PIECE_ROWS):
        rows = slice(r * OPROJ_PIECE_ROWS, (r + 1) * OPROJ_PIECE_ROWS)
        y = jnp.dot(y_ref[rows, :], w_ref[...], preferred_element_type=F32)
        o_ref[rows, :] = _layer_norm(alpha * x_ref[rows, :] + y, g_ref[...], b_ref[...])


OPROJ_PIECE_ROWS = 256


def _oproj(merged, w_o, x, g, b, *, alpha, tm=1024):
    m, d = x.shape
    return pl.pallas_call(
        functools.partial(_oproj_kernel, alpha=alpha),
        grid=(m // tm,),
        in_specs=[pl.BlockSpec((tm, d), lambda i: (i, 0)),
                  _resident((d, d)),
                  pl.BlockSpec((tm, d), lambda i: (i, 0)),
                  pl.BlockSpec((1, d), lambda i: (0, 0)),
                  pl.BlockSpec((1, d), lambda i: (0, 0))],
        out_specs=pl.BlockSpec((tm, d), lambda i: (i, 0)),
        out_shape=jax.ShapeDtypeStruct((m, d), F32),
        compiler_params=_params("parallel"),
    )(merged, w_o, x, g, b)


def _mlp_kernel(x_ref, wu_ref, wd_ref, g_ref, b_ref, o_ref, xb_ref, *, alpha):
    f = pl.program_id(1)
    nf = pl.num_programs(1)

    def step(first, last):
        for r in range(x_ref.shape[0] // MLP_PIECE_ROWS):
            rows = slice(r * MLP_PIECE_ROWS, (r + 1) * MLP_PIECE_ROWS)
            if first:
                xb_ref[rows, :] = x_ref[rows, :].astype(BF16)
            hid = jnp.dot(xb_ref[rows, :], wu_ref[...], preferred_element_type=F32)
            hid = jnp.square(jnp.maximum(hid, 0.0)).astype(BF16)
            y = jnp.dot(hid, wd_ref[...], preferred_element_type=F32)
            if not first:
                y = o_ref[rows, :] + y
            if last:
                y = _layer_norm(alpha * x_ref[rows, :] + y, g_ref[...], b_ref[...])
            o_ref[rows, :] = y

    pl.when(f == 0)(functools.partial(step, True, False))
    pl.when((f > 0) & (f < nf - 1))(functools.partial(step, False, False))
    pl.when(f == nf - 1)(functools.partial(step, False, True))


MLP_PIECE_ROWS = 256


def _mlp(x, w_up, w_down, g, b, *, alpha, tm=512, tf=1024):
    m, d = x.shape
    dff = w_up.shape[1]
    return pl.pallas_call(
        functools.partial(_mlp_kernel, alpha=alpha),
        grid=(m // tm, dff // tf),
        in_specs=[pl.BlockSpec((tm, d), lambda i, f: (i, 0)),
                  pl.BlockSpec((d, tf), lambda i, f: (0, f)),
                  pl.BlockSpec((tf, d), lambda i, f: (f, 0)),
                  pl.BlockSpec((1, d), lambda i, f: (0, 0)),
                  pl.BlockSpec((1, d), lambda i, f: (0, 0))],
        out_specs=pl.BlockSpec((tm, d), lambda i, f: (i, 0)),
        out_shape=jax.ShapeDtypeStruct((m, d), F32),
        scratch_shapes=[pltpu.VMEM((tm, d), BF16)],
        compiler_params=_params("parallel", "arbitrary"),
    )(x, w_up, w_down, g, b)


def kernel(x, w_in, lambda_q1, lambda_k1, lambda_q2, lambda_k2, attn_subln_g, sg_ln_g, sg_ln_b, sg_w_s, sg_b_s, b_gate, w_branch_attn, w_branch_sg, w_o, ln1_g, ln1_b, w_up, w_down, ln2_g, ln2_b):
    batch, seq, d = x.shape
    depth = w_in.shape[0]
    m = batch * seq
    alpha = (2.0 * depth) ** 0.25
    k_off = ATTN_WIDTH
    u_off = 3 * ATTN_WIDTH
    gate_off = u_off + 2 * SG_WIDTH
    row = lambda p: p.reshape(1, -1).astype(F32)

    i = jnp.arange(1, N_HEADS + 1, dtype=F32)
    slopes = jnp.exp2(-8.0 * i / N_HEADS)
    col_scale = jnp.where(jnp.arange(w_in.shape[2]) < k_off, LOG2E * QK_DIM ** -0.5, 1.0).astype(F32)

    xf = x.reshape(m, d)
    for l in range(depth):
        lambda_init = 0.8 - 0.6 * math.exp(-0.3 * l)
        w_in_b = (w_in[l] * col_scale).astype(BF16)
        qkv, xb = _proj_cast(xf, w_in_b, u_off)
        gd = SG_WIDTH // SG_GROUPS
        bs_full = jnp.repeat(sg_b_s[l].T.astype(F32), gd, axis=1)
        sgo = _sg_branch(xb, w_in_b, u_off, row(sg_ln_g[l]), row(sg_ln_b[l]), sg_w_s[l].astype(BF16), bs_full)

        attn = _attention(qkv.reshape(batch, seq, 3 * ATTN_WIDTH), LOG2E * slopes,
                          row(lambda_q1[l]), row(lambda_k1[l]), row(lambda_q2[l]), row(lambda_k2[l]),
                          row(attn_subln_g[l]), batch=batch, seq=seq, lambda_init=lambda_init)

        merged = _merge(attn.reshape(m, ATTN_WIDTH), sgo, xb, w_in_b, gate_off, row(b_gate[l]),
                        w_branch_attn[l].astype(BF16), w_branch_sg[l].astype(BF16))
        xf = _oproj(merged, w_o[l].astype(BF16), xf, row(ln1_g[l]), row(ln1_b[l]), alpha=alpha)
        xf = _mlp(xf, w_up[l].astype(BF16), w_down[l].astype(BF16), row(ln2_g[l]), row(ln2_b[l]), alpha=alpha)
    return xf.reshape(batch, seq, d)
```

```python
import functools
import math

import jax
import jax.numpy as jnp
from jax import lax
from jax.experimental import pallas as pl
from jax.experimental.pallas import tpu as pltpu

F32 = jnp.float32
BF16 = jnp.bfloat16

N_HEADS = 8
QK_DIM = 64
HEAD_DIM = 2 * QK_DIM
ATTN_WIDTH = N_HEADS * HEAD_DIM
SG_WIDTH = 1024
SG_GROUPS = 8
SG_CHUNK = 128
LN_EPS = 1e-5
INV_SQRT2 = 0.7071067811865476
LOG2E = 1.4426950408889634
LANES = 128

VMEM_LIMIT_BYTES = 60 * 1024 * 1024


def _params(*sem):
    return pltpu.CompilerParams(dimension_semantics=sem, vmem_limit_bytes=VMEM_LIMIT_BYTES)


def _layer_norm(y, g, b):
    mu = jnp.mean(y, axis=-1, keepdims=True)
    d = y - mu
    var = jnp.mean(d * d, axis=-1, keepdims=True)
    return d * lax.rsqrt(var + LN_EPS) * g + b


def _proj_cast_kernel(x_ref, w_ref, o_ref, xb_ref):
    xb = x_ref[...].astype(BF16)
    xb_ref[...] = xb
    o_ref[...] = jnp.dot(xb, w_ref[...], preferred_element_type=F32).astype(o_ref.dtype)


def _resident(shape, index=None):
    index = (0,) * len(shape) if index is None else index
    return pl.BlockSpec(shape, lambda i: index, pipeline_mode=pl.Buffered(1))


def _proj_cast(x, w, n, *, tm=512):
    m, k = x.shape
    return pl.pallas_call(
        _proj_cast_kernel,
        grid=(m // tm,),
        in_specs=[pl.BlockSpec((tm, k), lambda i: (i, 0)), _resident((k, n))],
        out_specs=[pl.BlockSpec((tm, n), lambda i: (i, 0)),
                   pl.BlockSpec((tm, k), lambda i: (i, 0))],
        out_shape=[jax.ShapeDtypeStruct((m, n), BF16),
                   jax.ShapeDtypeStruct((m, k), BF16)],
        compiler_params=_params("parallel"),
    )(x, w)


SG_PIECE_ROWS = 256


def _sg_branch_kernel(x_ref, wu_ref, wv_ref, g_ref, b_ref, ws_ref, bs_ref, o_ref, u_ref, vn_ref):
    gelu = lambda y: 0.5 * y * (1.0 + lax.erf(y * INV_SQRT2))
    pieces = x_ref.shape[0] // SG_PIECE_ROWS
    chunks = SG_PIECE_ROWS // SG_CHUNK
    gd = SG_WIDTH // SG_GROUPS
    for piece in range(pieces):
        rows = slice(piece * SG_PIECE_ROWS, (piece + 1) * SG_PIECE_ROWS)
        u_ref[rows, :] = gelu(jnp.dot(x_ref[rows, :], wu_ref[...], preferred_element_type=F32))
        v = gelu(jnp.dot(x_ref[rows, :], wv_ref[...], preferred_element_type=F32))
        vn_ref[rows, :] = _layer_norm(v, g_ref[...], b_ref[...]).astype(BF16)
    for piece in range(pieces):
        r0 = piece * SG_PIECE_ROWS
        for g in range(SG_GROUPS):
            cols = slice(g * gd, (g + 1) * gd)
            vg = jnp.concatenate([vn_ref[r0 + c * SG_CHUNK:r0 + (c + 1) * SG_CHUNK, cols]
                                  for c in range(chunks)], axis=1)
            mixed = jnp.dot(ws_ref[g], vg, preferred_element_type=F32)
            for c in range(chunks):
                mx = mixed[:, c * gd:(c + 1) * gd] + bs_ref[:, cols]
                rows = slice(r0 + c * SG_CHUNK, r0 + (c + 1) * SG_CHUNK)
                o_ref[rows, cols] = (u_ref[rows, cols] * mx).astype(o_ref.dtype)


def _sg_branch(x, w_in, u_off, ln_g, ln_b, w_s, bs_full, *, tm=512):
    m, k = x.shape
    ju = u_off // SG_WIDTH
    return pl.pallas_call(
        _sg_branch_kernel,
        grid=(m // tm,),
        in_specs=[pl.BlockSpec((tm, k), lambda i: (i, 0)),
                  _resident((k, SG_WIDTH), (0, ju)),
                  _resident((k, SG_WIDTH), (0, ju + 1)),
                  pl.BlockSpec((1, SG_WIDTH), lambda i: (0, 0)),
                  pl.BlockSpec((1, SG_WIDTH), lambda i: (0, 0)),
                  pl.BlockSpec(w_s.shape, lambda i: (0, 0, 0)),
                  pl.BlockSpec(bs_full.shape, lambda i: (0, 0))],
        out_specs=pl.BlockSpec((tm, SG_WIDTH), lambda i: (i, 0)),
        out_shape=jax.ShapeDtypeStruct((m, SG_WIDTH), BF16),
        scratch_shapes=[pltpu.VMEM((tm, SG_WIDTH), F32), pltpu.VMEM((tm, SG_WIDTH), BF16)],
        compiler_params=_params("parallel"),
    )(x, w_in, w_in, ln_g, ln_b, w_s, bs_full)


FEAT_ROWS = 16
ONES_ROWS = 16
BAND_GROUP = 256
BEFORE, AFTER, DIAG = 0, 1, 2
NORM_SLACK = 1.02
BOUND_LIMIT = 64.0
HEADROOM = 29.0
ZERO_EXPONENT = 155.0


def _pos_lo_bits(t):
    return max(0, t.bit_length() - 9)


def _feature_table(c, t):
    radix = float(1 << _pos_lo_bits(t))
    c1 = c.astype(BF16).astype(F32)
    c2 = (c - c1).astype(BF16).astype(F32)
    c3 = (c - c1 - c2).astype(BF16).astype(F32)
    nh = c.shape[0]
    z = jnp.zeros((nh, LANES), F32)
    ind = lambda lo: jnp.zeros((nh, LANES), F32).at[:, lo:lo + 3].set(1.0)
    pieces = jnp.stack([radix * c1, radix * c2, radix * c3, c1, c2, c3], axis=1)
    at = lambda lo: z.at[:, lo:lo + 6].set(pieces)
    rows = [ind(0), ind(3), ind(6), ind(9), at(12),
            at(6), at(0), ind(12), ind(15)]
    return jnp.stack(rows + [z] * (FEAT_ROWS - len(rows)), axis=1)


def _near_rows(c, t, nk):
    rows = -(-int(math.ceil((HEADROOM + ZERO_EXPONENT) / c - 1.0)) // LANES) * LANES
    return rows if rows <= t and nk >= 3 else None


def _max_sq_norms(x_ref, groups=8):
    n = x_ref.shape[1] // groups
    g = jnp.square(x_ref[0, 0:n, :].astype(F32))
    for i in range(1, groups):
        g = jnp.maximum(g, jnp.square(x_ref[0, i * n:(i + 1) * n, :].astype(F32)))
    lane = lax.broadcasted_iota(jnp.int32, g.shape, 1)
    out = []
    for mask in (lane < QK_DIM, lane >= QK_DIM):
        n2 = jnp.sum(jnp.where(mask, g, 0.0), axis=1, keepdims=True)
        out.append(jnp.max(n2, axis=0, keepdims=True))
    return out


def _attn_kernel(c_ref, lq1_ref, lk1_ref, lq2_ref, lk2_ref, gain_ref, tab_ref,
                 q_ref, k_ref, v_ref, o_ref,
                 kf_ref, vt_ref, wq_ref, tbase_ref, tdiag_ref, mref_ref, m_ref, lam_ref, acc_ref, p_ref,
                 out_ref, flag_ref,
                 *, t, lambda_init, head_near_rows):
    h = pl.program_id(0)
    bi = pl.program_id(1)
    qi = pl.program_id(2)
    nk = k_ref.shape[1] // t
    c = c_ref[h]

    @pl.when((h == 0) & (bi == 0) & (qi == 0))
    def _once():
        key = lax.broadcasted_iota(jnp.int32, (t, t), 0)
        qry = lax.broadcasted_iota(jnp.int32, (t, t), 1)
        tbase_ref[...] = jnp.abs(key - qry).astype(F32)
        lam = (jnp.exp(jnp.sum(lq1_ref[...] * lk1_ref[...], axis=-1, keepdims=True))
               - jnp.exp(jnp.sum(lq2_ref[...] * lk2_ref[...], axis=-1, keepdims=True))
               + lambda_init)
        lam_ref[...] = jnp.broadcast_to(lam, lam_ref.shape)

    @pl.when((bi == 0) & (qi == 0))
    def _per_head():
        tdiag_ref[...] = -c * tbase_ref[...]
        pos = lax.broadcasted_iota(jnp.int32, (t, LANES), 0)
        lo_bits = _pos_lo_bits(t)
        split = lambda x: ((x >> lo_bits).astype(F32), (x & ((1 << lo_bits) - 1)).astype(F32))
        hi, lo = split(pos)
        rhi, rlo = split(t - pos)
        tab = tab_ref[0]
        kfeat = (hi * tab[0:1] + lo * tab[1:2] + rhi * tab[2:3] + rlo * tab[3:4] + tab[4:5]).astype(BF16)
        q_before = -(tab[5:6] + hi * tab[7:8] + lo * tab[8:9])
        q_after = -(tab[6:7] + rhi * tab[7:8] + rlo * tab[8:9])
        for j in range(nk):
            kf_ref[j * t:(j + 1) * t, LANES:2 * LANES] = kfeat
            vt_ref[j, HEAD_DIM:, :] = jnp.ones((ONES_ROWS, t), BF16)
        for half in range(2):
            rows = slice(half * t, (half + 1) * t)
            wq_ref[BEFORE, rows, LANES:2 * LANES] = q_before.astype(BF16)
            wq_ref[AFTER, rows, LANES:2 * LANES] = q_after.astype(BF16)
            wq_ref[DIAG, rows, LANES:2 * LANES] = jnp.zeros((t, LANES), BF16)

    @pl.when(qi == 0)
    def _per_batch_and_head():
        kf_ref[:, 0:LANES] = k_ref[0]
        for j in range(nk):
            vt_ref[j, 0:HEAD_DIM, :] = v_ref[0, j * t:(j + 1) * t, :].astype(F32).T.astype(BF16)
        qn = _max_sq_norms(q_ref)
        kn = _max_sq_norms(k_ref)
        bounds = [jnp.sqrt(a * b) * NORM_SLACK for a, b in zip(qn, kn)]
        for half in range(2):
            mref_ref[:, half * t:(half + 1) * t] = jnp.broadcast_to(bounds[half] - HEADROOM, (1, t))
        flag_ref[0] = (jnp.max(jnp.maximum(bounds[0], bounds[1])) <= BOUND_LIMIT).astype(jnp.int32)

    q = q_ref[0, pl.ds(pl.multiple_of(qi * t, t), t), :].astype(F32)
    lane = lax.broadcasted_iota(jnp.int32, q.shape, 1)
    q1 = jnp.where(lane < QK_DIM, q, 0.0).astype(BF16)
    q2 = jnp.where(lane >= QK_DIM, q, 0.0).astype(BF16)
    for variant in (BEFORE, AFTER, DIAG):
        wq_ref[variant, 0:t, 0:LANES] = q1
        wq_ref[variant, t:2 * t, 0:LANES] = q2
    fast_ok = flag_ref[0] == 1

    def place(r):
        j = qi + r
        after = j < nk
        return (jnp.where(after, j, j - nk), jnp.where(after, AFTER, BEFORE),
                jnp.where(after, r, nk - r))

    def block(r, rows=(0, t), qcols=(0, t)):
        row0, n = rows
        q0, nq = qcols
        jw, variant, dist = (qi, DIAG, 0) if isinstance(r, int) and r == 0 else place(r)
        kt = kf_ref[pl.ds(pl.multiple_of(jw * t + row0, LANES), n), :]
        if nq == t:
            wq = wq_ref[variant]
        else:
            wq = jnp.concatenate([wq_ref[variant, q0:q0 + nq, :], wq_ref[variant, t + q0:t + q0 + nq, :]], axis=0)
        s = lax.dot_general(kt, wq, (((1,), (1,)), ((), ())), preferred_element_type=F32)
        if isinstance(r, int) and r == 0:
            bias = tdiag_ref[row0:row0 + n, q0:q0 + nq]
            return jnp.concatenate([s[:, 0:nq] + bias, s[:, nq:2 * nq] + bias], axis=1), 0.0, jw
        return s, -c * ((dist - 1) * t).astype(F32), jw


    def fixed_reference(near_rows):
        if near_rows is None:
            plan = [((0, t), [(r, (0, t)) for r in range(nk)])]
        else:
            plan = []
            for q0 in range(0, t, BAND_GROUP):
                q1 = q0 + BAND_GROUP
                lo, hi = max(0, q0 - near_rows), min(t, q1 + near_rows)
                visits = [(0, (lo, hi - lo))]
                n_after = min(t, near_rows - (t - q1))
                n_before = min(t, near_rows - q0)
                if n_after > 0:
                    visits.append((1, (0, n_after)))
                if n_before > 0:
                    visits.append((nk - 1, (t - n_before, n_before)))
                plan.append(((q0, BAND_GROUP), visits))
        for (q0, nq), visits in plan:
            m0 = jnp.concatenate([mref_ref[:, q0:q0 + nq], mref_ref[:, t + q0:t + q0 + nq]], axis=1)
            l = None
            slots = []
            first = 0
            for r, rows in visits:
                s, shift, jw = block(r, rows, (q0, nq))
                p = jnp.exp2(s - (m0 - shift))
                ps = jnp.sum(p, axis=0, keepdims=True)
                l = ps if l is None else l + ps
                p_ref[first:first + rows[1], 0:2 * nq] = p.astype(BF16)
                slots.append((first, jw, rows))
                first += rows[1]
            rl = 1.0 / l
            r1 = rl[:, 0:nq].astype(BF16)
            r2 = (lam_ref[:, 0:1] * rl[:, nq:2 * nq]).astype(BF16)
            out = None
            for first, jw, (row0, n) in slots:
                w = p_ref[first:first + n, 0:nq] * r1 - p_ref[first:first + n, nq:2 * nq] * r2
                pv = jnp.dot(vt_ref[jw, 0:HEAD_DIM, row0:row0 + n], w, preferred_element_type=F32)
                out = pv if out is None else out + pv
            out_ref[:, q0:q0 + nq] = out

    for near_rows in sorted(set(head_near_rows), key=lambda v: (v is None, v)):
        in_class = functools.reduce(jnp.logical_or,
                                    [h == hh for hh, v in enumerate(head_near_rows) if v == near_rows])
        pl.when(fast_ok & in_class)(functools.partial(fixed_reference, near_rows))

    @pl.when(jnp.logical_not(fast_ok))
    def _running_max():
        s, _, jw = block(0)
        m_ref[...] = jnp.max(s, axis=0, keepdims=True)
        acc_ref[...] = jnp.dot(vt_ref[jw], jnp.exp2(s - m_ref[...]).astype(BF16),
                               preferred_element_type=F32)

        def fold(r, carry):
            s, shift, jw = block(r)
            m_old = m_ref[...]
            m_new = jnp.maximum(m_old, jnp.max(s, axis=0, keepdims=True) + shift)
            pv = jnp.dot(vt_ref[jw], jnp.exp2(s - (m_new - shift)).astype(BF16),
                         preferred_element_type=F32)
            acc_ref[...] = jnp.exp2(m_old - m_new) * acc_ref[...] + pv
            m_ref[...] = m_new
            return carry

        lax.fori_loop(1, nk, fold, 0)
        rl = 1.0 / acc_ref[HEAD_DIM:HEAD_DIM + 1, :]
        out_ref[...] = (acc_ref[0:HEAD_DIM, 0:t] * rl[:, 0:t]
                        - acc_ref[0:HEAD_DIM, t:2 * t] * (lam_ref[:, 0:1] * rl[:, t:2 * t]))

    o = out_ref[...]
    ms = jnp.mean(o * o, axis=0, keepdims=True)
    o = o * lax.rsqrt(ms + LN_EPS)
    o = o.T * gain_ref[...] * (1.0 - lambda_init)
    o_ref[0] = o.astype(o_ref.dtype)


def _attention(qkv, c, lq1, lk1, lq2, lk2, gain, *, batch, seq, lambda_init, t=1024):
    nh = N_HEADS
    vec = lambda n: pl.BlockSpec((1, n), lambda h, b, i: (0, 0))
    near = tuple(_near_rows(LOG2E * 2.0 ** (-8.0 * (i + 1) / nh), t, seq // t) for i in range(nh))
    return pl.pallas_call(
        functools.partial(_attn_kernel, t=t, lambda_init=lambda_init, head_near_rows=near),
        grid=(nh, batch, seq // t),
        in_specs=[pl.BlockSpec(memory_space=pltpu.SMEM),
                  vec(QK_DIM), vec(QK_DIM), vec(QK_DIM), vec(QK_DIM), vec(HEAD_DIM),
                  pl.BlockSpec((1, FEAT_ROWS, LANES), lambda h, b, i: (h, 0, 0)),
                  pl.BlockSpec((1, seq, HEAD_DIM), lambda h, b, i: (b, 0, h)),
                  pl.BlockSpec((1, seq, HEAD_DIM), lambda h, b, i: (b, 0, nh + h)),
                  pl.BlockSpec((1, seq, HEAD_DIM), lambda h, b, i: (b, 0, 2 * nh + h))],
        out_specs=pl.BlockSpec((1, t, HEAD_DIM), lambda h, b, i: (b, i, h)),
        out_shape=jax.ShapeDtypeStruct((batch, seq, ATTN_WIDTH), BF16),
        scratch_shapes=[pltpu.VMEM((seq, 2 * LANES), BF16),
                        pltpu.VMEM((seq // t, HEAD_DIM + ONES_ROWS, t), BF16),
                        pltpu.VMEM((3, 2 * t, 2 * LANES), BF16),
                        pltpu.VMEM((t, t), F32),
                        pltpu.VMEM((t, t), F32),
                        pltpu.VMEM((1, 2 * t), F32),
                        pltpu.VMEM((1, 2 * t), F32),
                        pltpu.VMEM((1, LANES), F32),
                        pltpu.VMEM((HEAD_DIM + ONES_ROWS, 2 * t), F32),
                        pltpu.VMEM((seq, 2 * t), BF16),
                        pltpu.VMEM((HEAD_DIM, t), F32),
                        pltpu.SMEM((1,), jnp.int32)],
        compiler_params=_params("arbitrary", "arbitrary", "arbitrary"),
    )(c, lq1, lk1, lq2, lk2, gain, _feature_table(c, t), qkv, qkv, qkv)


def _merge_kernel(a_ref, s_ref, x_ref, wa_ref, ws_ref, wg0_ref, wg1_ref, bg0_ref, bg1_ref, o_ref):
    x = x_ref[...]
    g0 = jax.nn.sigmoid(jnp.dot(x, wg0_ref[...], preferred_element_type=F32) + bg0_ref[...])
    g1 = jax.nn.sigmoid(jnp.dot(x, wg1_ref[...], preferred_element_type=F32) + bg1_ref[...])
    ya = jnp.dot(a_ref[...], wa_ref[...], preferred_element_type=F32)
    ys = jnp.dot(s_ref[...], ws_ref[...], preferred_element_type=F32)
    o_ref[...] = (g0 * ya + g1 * ys).astype(o_ref.dtype)


def _merge(attn, sgo, x, w_in, gate_off, b_gate, w_ba, w_bs, *, tm=1024, tn=512):
    m, d = x.shape
    nb = d // tn
    j0 = gate_off // tn
    return pl.pallas_call(
        _merge_kernel,
        grid=(m // tm, d // tn),
        in_specs=[pl.BlockSpec((tm, attn.shape[1]), lambda i, j: (i, 0)),
                  pl.BlockSpec((tm, sgo.shape[1]), lambda i, j: (i, 0)),
                  pl.BlockSpec((tm, d), lambda i, j: (i, 0)),
                  pl.BlockSpec((w_ba.shape[0], tn), lambda i, j: (0, j)),
                  pl.BlockSpec((w_bs.shape[0], tn), lambda i, j: (0, j)),
                  pl.BlockSpec((d, tn), lambda i, j: (0, j0 + j)),
                  pl.BlockSpec((d, tn), lambda i, j: (0, j0 + nb + j)),
                  pl.BlockSpec((1, tn), lambda i, j: (0, j)),
                  pl.BlockSpec((1, tn), lambda i, j: (0, nb + j))],
        out_specs=pl.BlockSpec((tm, tn), lambda i, j: (i, j)),
        out_shape=jax.ShapeDtypeStruct((m, d), BF16),
        compiler_params=_params("parallel", "arbitrary"),
    )(attn, sgo, x, w_ba, w_bs, w_in, w_in, b_gate, b_gate)


def _oproj_kernel(y_ref, w_ref, x_ref, g_ref, b_ref, o_ref, *, alpha):
    for r in range(y_ref.shape[0] // OPROJ_PIECE_ROWS):
        rows = slice(r * OPROJ_PIECE_ROWS, (r + 1) * OPROJ_PIECE_ROWS)
        y = jnp.dot(y_ref[rows, :], w_ref[...], preferred_element_type=F32)
        o_ref[rows, :] = _layer_norm(alpha * x_ref[rows, :] + y, g_ref[...], b_ref[...])


OPROJ_PIECE_ROWS = 256


def _oproj(merged, w_o, x, g, b, *, alpha, tm=1024):
    m, d = x.shape
    return pl.pallas_call(
        functools.partial(_oproj_kernel, alpha=alpha),
        grid=(m // tm,),
        in_specs=[pl.BlockSpec((tm, d), lambda i: (i, 0)),
                  _resident((d, d)),
                  pl.BlockSpec((tm, d), lambda i: (i, 0)),
                  pl.BlockSpec((1, d), lambda i: (0, 0)),
                  pl.BlockSpec((1, d), lambda i: (0, 0))],
        out_specs=pl.BlockSpec((tm, d), lambda i: (i, 0)),
        out_shape=jax.ShapeDtypeStruct((m, d), F32),
        compiler_params=_params("parallel"),
    )(merged, w_o, x, g, b)


def _mlp_kernel(x_ref, wu_ref, wd_ref, g_ref, b_ref, o_ref, xb_ref, *, alpha):
    f = pl.program_id(1)
    nf = pl.num_programs(1)

    def step(first, last):
        for r in range(x_ref.shape[0] // MLP_PIECE_ROWS):
            rows = slice(r * MLP_PIECE_ROWS, (r + 1) * MLP_PIECE_ROWS)
            if first:
                xb_ref[rows, :] = x_ref[rows, :].astype(BF16)
            hid = jnp.dot(xb_ref[rows, :], wu_ref[...], preferred_element_type=F32)
            hid = jnp.square(jnp.maximum(hid, 0.0)).astype(BF16)
            y = jnp.dot(hid, wd_ref[...], preferred_element_type=F32)
            if not first:
                y = o_ref[rows, :] + y
            if last:
                y = _layer_norm(alpha * x_ref[rows, :] + y, g_ref[...], b_ref[...])
            o_ref[rows, :] = y

    pl.when(f == 0)(functools.partial(step, True, False))
    pl.when((f > 0) & (f < nf - 1))(functools.partial(step, False, False))
    pl.when(f == nf - 1)(functools.partial(step, False, True))


MLP_PIECE_ROWS = 256


def _mlp(x, w_up, w_down, g, b, *, alpha, tm=512, tf=1024):
    m, d = x.shape
    dff = w_up.shape[1]
    return pl.pallas_call(
        functools.partial(_mlp_kernel, alpha=alpha),
        grid=(m // tm, dff // tf),
        in_specs=[pl.BlockSpec((tm, d), lambda i, f: (i, 0)),
                  pl.BlockSpec((d, tf), lambda i, f: (0, f)),
                  pl.BlockSpec((tf, d), lambda i, f: (f, 0)),
                  pl.BlockSpec((1, d), lambda i, f: (0, 0)),
                  pl.BlockSpec((1, d), lambda i, f: (0, 0))],
        out_specs=pl.BlockSpec((tm, d), lambda i, f: (i, 0)),
        out_shape=jax.ShapeDtypeStruct((m, d), F32),
        scratch_shapes=[pltpu.VMEM((tm, d), BF16)],
        compiler_params=_params("parallel", "arbitrary"),
    )(x, w_up, w_down, g, b)


def kernel(x, w_in, lambda_q1, lambda_k1, lambda_q2, lambda_k2, attn_subln_g, sg_ln_g, sg_ln_b, sg_w_s, sg_b_s, b_gate, w_branch_attn, w_branch_sg, w_o, ln1_g, ln1_b, w_up, w_down, ln2_g, ln2_b):
    batch, seq, d = x.shape
    depth = w_in.shape[0]
    m = batch * seq
    alpha = (2.0 * depth) ** 0.25
    k_off = ATTN_WIDTH
    u_off = 3 * ATTN_WIDTH
    gate_off = u_off + 2 * SG_WIDTH
    row = lambda p: p.reshape(1, -1).astype(F32)

    i = jnp.arange(1, N_HEADS + 1, dtype=F32)
    slopes = jnp.exp2(-8.0 * i / N_HEADS)
    col_scale = jnp.where(jnp.arange(w_in.shape[2]) < k_off, LOG2E * QK_DIM ** -0.5, 1.0).astype(F32)

    xf = x.reshape(m, d)
    for l in range(depth):
        lambda_init = 0.8 - 0.6 * math.exp(-0.3 * l)
        w_in_b = (w_in[l] * col_scale).astype(BF16)
        qkv, xb = _proj_cast(xf, w_in_b, u_off)
        gd = SG_WIDTH // SG_GROUPS
        bs_full = jnp.repeat(sg_b_s[l].T.astype(F32), gd, axis=1)
        sgo = _sg_branch(xb, w_in_b, u_off, row(sg_ln_g[l]), row(sg_ln_b[l]), sg_w_s[l].astype(BF16), bs_full)

        attn = _attention(qkv.reshape(batch, seq, 3 * ATTN_WIDTH), LOG2E * slopes,
                          row(lambda_q1[l]), row(lambda_k1[l]), row(lambda_q2[l]), row(lambda_k2[l]),
                          row(attn_subln_g[l]), batch=batch, seq=seq, lambda_init=lambda_init)

        merged = _merge(attn.reshape(m, ATTN_WIDTH), sgo, xb, w_in_b, gate_off, row(b_gate[l]),
                        w_branch_attn[l].astype(BF16), w_branch_sg[l].astype(BF16))
        xf = _oproj(merged, w_o[l].astype(BF16), xf, row(ln1_g[l]), row(ln1_b[l]), alpha=alpha)
        xf = _mlp(xf, w_up[l].astype(BF16), w_down[l].astype(BF16), row(ln2_g[l]), row(ln2_b[l]), alpha=alpha)
    return xf.reshape(batch, seq, d)
```

```python
import functools
import math

import jax
import jax.numpy as jnp
from jax import lax
from jax.experimental import pallas as pl
from jax.experimental.pallas import tpu as pltpu

F32 = jnp.float32
BF16 = jnp.bfloat16

N_HEADS = 8
QK_DIM = 64
HEAD_DIM = 2 * QK_DIM
ATTN_WIDTH = N_HEADS * HEAD_DIM
SG_WIDTH = 1024
SG_GROUPS = 8
SG_CHUNK = 128
LN_EPS = 1e-5
INV_SQRT2 = 0.7071067811865476
LOG2E = 1.4426950408889634
LANES = 128

VMEM_LIMIT_BYTES = 60 * 1024 * 1024


def _params(*sem):
    return pltpu.CompilerParams(dimension_semantics=sem, vmem_limit_bytes=VMEM_LIMIT_BYTES)


def _layer_norm(y, g, b):
    mu = jnp.mean(y, axis=-1, keepdims=True)
    d = y - mu
    var = jnp.mean(d * d, axis=-1, keepdims=True)
    return d * lax.rsqrt(var + LN_EPS) * g + b


def _proj_cast_kernel(x_ref, w_ref, *rest):
    n = (len(rest) - 2) // 2
    o_ref, xb_ref = rest[n], rest[n + 1]
    xb = x_ref[...].astype(BF16)
    xb_ref[...] = xb
    o_ref[...] = jnp.dot(xb, w_ref[...], preferred_element_type=F32).astype(o_ref.dtype)
    _cast_chunks(rest[:n], rest[n + 2:])


def _resident(shape, index=None):
    index = (0,) * len(shape) if index is None else index
    return pl.BlockSpec(shape, lambda i: index, pipeline_mode=pl.Buffered(1))


def _chunk_cast_specs(arrays, nsteps, step_of):
    ins, shapes = [], []
    for a in arrays:
        ins.append(pl.BlockSpec((a.shape[0] // nsteps, a.shape[1]), lambda *g: (step_of(*g), 0)))
        shapes.append(jax.ShapeDtypeStruct(a.shape, BF16))
    return ins, list(ins), shapes


def _cast_chunks(src_refs, dst_refs):
    for src, dst in zip(src_refs, dst_refs, strict=True):
        dst[...] = src[...].astype(BF16)


def _proj_cast(x, w, n, cast_later, *, tm=512):
    m, k = x.shape
    cast_in, cast_out, cast_shapes = _chunk_cast_specs(cast_later, m // tm, lambda i: i)
    out = pl.pallas_call(
        _proj_cast_kernel,
        grid=(m // tm,),
        in_specs=[pl.BlockSpec((tm, k), lambda i: (i, 0)), _resident((k, n))] + cast_in,
        out_specs=[pl.BlockSpec((tm, n), lambda i: (i, 0)),
                   pl.BlockSpec((tm, k), lambda i: (i, 0))] + cast_out,
        out_shape=[jax.ShapeDtypeStruct((m, n), BF16),
                   jax.ShapeDtypeStruct((m, k), BF16)] + cast_shapes,
        compiler_params=_params("arbitrary"),
    )(x, w, *cast_later)
    return out[0], out[1], out[2:]


SG_PIECE_ROWS = 256


def _sg_branch_kernel(x_ref, wu_ref, wv_ref, g_ref, b_ref, ws_ref, bs_ref, o_ref, u_ref, vn_ref):
    gelu = lambda y: 0.5 * y * (1.0 + lax.erf(y * INV_SQRT2))
    pieces = x_ref.shape[0] // SG_PIECE_ROWS
    chunks = SG_PIECE_ROWS // SG_CHUNK
    gd = SG_WIDTH // SG_GROUPS
    for piece in range(pieces):
        rows = slice(piece * SG_PIECE_ROWS, (piece + 1) * SG_PIECE_ROWS)
        u_ref[rows, :] = gelu(jnp.dot(x_ref[rows, :], wu_ref[...], preferred_element_type=F32))
        v = gelu(jnp.dot(x_ref[rows, :], wv_ref[...], preferred_element_type=F32))
        vn_ref[rows, :] = _layer_norm(v, g_ref[...], b_ref[...]).astype(BF16)
    for piece in range(pieces):
        r0 = piece * SG_PIECE_ROWS
        for g in range(SG_GROUPS):
            cols = slice(g * gd, (g + 1) * gd)
            vg = jnp.concatenate([vn_ref[r0 + c * SG_CHUNK:r0 + (c + 1) * SG_CHUNK, cols]
                                  for c in range(chunks)], axis=1)
            mixed = jnp.dot(ws_ref[g], vg, preferred_element_type=F32)
            for c in range(chunks):
                mx = mixed[:, c * gd:(c + 1) * gd] + bs_ref[:, cols]
                rows = slice(r0 + c * SG_CHUNK, r0 + (c + 1) * SG_CHUNK)
                o_ref[rows, cols] = (u_ref[rows, cols] * mx).astype(o_ref.dtype)


def _sg_branch(x, w_in, u_off, ln_g, ln_b, w_s, bs_full, *, tm=512):
    m, k = x.shape
    ju = u_off // SG_WIDTH
    return pl.pallas_call(
        _sg_branch_kernel,
        grid=(m // tm,),
        in_specs=[pl.BlockSpec((tm, k), lambda i: (i, 0)),
                  _resident((k, SG_WIDTH), (0, ju)),
                  _resident((k, SG_WIDTH), (0, ju + 1)),
                  pl.BlockSpec((1, SG_WIDTH), lambda i: (0, 0)),
                  pl.BlockSpec((1, SG_WIDTH), lambda i: (0, 0)),
                  pl.BlockSpec(w_s.shape, lambda i: (0, 0, 0)),
                  pl.BlockSpec(bs_full.shape, lambda i: (0, 0))],
        out_specs=pl.BlockSpec((tm, SG_WIDTH), lambda i: (i, 0)),
        out_shape=jax.ShapeDtypeStruct((m, SG_WIDTH), BF16),
        scratch_shapes=[pltpu.VMEM((tm, SG_WIDTH), F32), pltpu.VMEM((tm, SG_WIDTH), BF16)],
        compiler_params=_params("parallel"),
    )(x, w_in, w_in, ln_g, ln_b, w_s, bs_full)


FEAT_ROWS = 16
ONES_ROWS = 16
BAND_GROUP = 256
BEFORE, AFTER, DIAG = 0, 1, 2
NORM_SLACK = 1.02
BOUND_LIMIT = 64.0
HEADROOM = 29.0
ZERO_EXPONENT = 155.0


def _pos_lo_bits(t):
    return max(0, t.bit_length() - 9)


def _feature_table(c, t):
    radix = float(1 << _pos_lo_bits(t))
    c1 = c.astype(BF16).astype(F32)
    c2 = (c - c1).astype(BF16).astype(F32)
    c3 = (c - c1 - c2).astype(BF16).astype(F32)
    nh = c.shape[0]
    z = jnp.zeros((nh, LANES), F32)
    ind = lambda lo: jnp.zeros((nh, LANES), F32).at[:, lo:lo + 3].set(1.0)
    pieces = jnp.stack([radix * c1, radix * c2, radix * c3, c1, c2, c3], axis=1)
    at = lambda lo: z.at[:, lo:lo + 6].set(pieces)
    rows = [ind(0), ind(3), ind(6), ind(9), at(12),
            at(6), at(0), ind(12), ind(15)]
    return jnp.stack(rows + [z] * (FEAT_ROWS - len(rows)), axis=1)


def _near_rows(c, t, nk):
    rows = -(-int(math.ceil((HEADROOM + ZERO_EXPONENT) / c - 1.0)) // LANES) * LANES
    return rows if rows <= t and nk >= 3 else None


def _max_sq_norms(x_ref, groups=8):
    n = x_ref.shape[1] // groups
    g = jnp.square(x_ref[0, 0:n, :].astype(F32))
    for i in range(1, groups):
        g = jnp.maximum(g, jnp.square(x_ref[0, i * n:(i + 1) * n, :].astype(F32)))
    lane = lax.broadcasted_iota(jnp.int32, g.shape, 1)
    out = []
    for mask in (lane < QK_DIM, lane >= QK_DIM):
        n2 = jnp.sum(jnp.where(mask, g, 0.0), axis=1, keepdims=True)
        out.append(jnp.max(n2, axis=0, keepdims=True))
    return out


def _attn_kernel(c_ref, lq1_ref, lk1_ref, lq2_ref, lk2_ref, gain_ref, tab_ref,
                 q_ref, k_ref, v_ref, o_ref,
                 kf_ref, vt_ref, wq_ref, tbase_ref, tdiag_ref, mref_ref, m_ref, lam_ref, acc_ref, p_ref,
                 out_ref, flag_ref,
                 *, t, lambda_init, head_near_rows):
    h = pl.program_id(0)
    bi = pl.program_id(1)
    qi = pl.program_id(2)
    nk = k_ref.shape[1] // t
    c = c_ref[h]

    @pl.when((h == 0) & (bi == 0) & (qi == 0))
    def _once():
        key = lax.broadcasted_iota(jnp.int32, (t, t), 0)
        qry = lax.broadcasted_iota(jnp.int32, (t, t), 1)
        tbase_ref[...] = jnp.abs(key - qry).astype(F32)
        lam = (jnp.exp(jnp.sum(lq1_ref[...] * lk1_ref[...], axis=-1, keepdims=True))
               - jnp.exp(jnp.sum(lq2_ref[...] * lk2_ref[...], axis=-1, keepdims=True))
               + lambda_init)
        lam_ref[...] = jnp.broadcast_to(lam, lam_ref.shape)

    @pl.when((bi == 0) & (qi == 0))
    def _per_head():
        tdiag_ref[...] = -c * tbase_ref[...]
        pos = lax.broadcasted_iota(jnp.int32, (t, LANES), 0)
        lo_bits = _pos_lo_bits(t)
        split = lambda x: ((x >> lo_bits).astype(F32), (x & ((1 << lo_bits) - 1)).astype(F32))
        hi, lo = split(pos)
        rhi, rlo = split(t - pos)
        tab = tab_ref[0]
        kfeat = (hi * tab[0:1] + lo * tab[1:2] + rhi * tab[2:3] + rlo * tab[3:4] + tab[4:5]).astype(BF16)
        q_before = -(tab[5:6] + hi * tab[7:8] + lo * tab[8:9])
        q_after = -(tab[6:7] + rhi * tab[7:8] + rlo * tab[8:9])
        for j in range(nk):
            kf_ref[j * t:(j + 1) * t, LANES:2 * LANES] = kfeat
            vt_ref[j, HEAD_DIM:, :] = jnp.ones((ONES_ROWS, t), BF16)
        for half in range(2):
            rows = slice(half * t, (half + 1) * t)
            wq_ref[BEFORE, rows, LANES:2 * LANES] = q_before.astype(BF16)
            wq_ref[AFTER, rows, LANES:2 * LANES] = q_after.astype(BF16)
            wq_ref[DIAG, rows, LANES:2 * LANES] = jnp.zeros((t, LANES), BF16)

    @pl.when(qi == 0)
    def _per_batch_and_head():
        kf_ref[:, 0:LANES] = k_ref[0]
        for j in range(nk):
            vt_ref[j, 0:HEAD_DIM, :] = v_ref[0, j * t:(j + 1) * t, :].astype(F32).T.astype(BF16)
        qn = _max_sq_norms(q_ref)
        kn = _max_sq_norms(k_ref)
        bounds = [jnp.sqrt(a * b) * NORM_SLACK for a, b in zip(qn, kn)]
        for half in range(2):
            mref_ref[:, half * t:(half + 1) * t] = jnp.broadcast_to(bounds[half] - HEADROOM, (1, t))
        flag_ref[0] = (jnp.max(jnp.maximum(bounds[0], bounds[1])) <= BOUND_LIMIT).astype(jnp.int32)

    q = q_ref[0, pl.ds(pl.multiple_of(qi * t, t), t), :].astype(F32)
    lane = lax.broadcasted_iota(jnp.int32, q.shape, 1)
    q1 = jnp.where(lane < QK_DIM, q, 0.0).astype(BF16)
    q2 = jnp.where(lane >= QK_DIM, q, 0.0).astype(BF16)
    for variant in (BEFORE, AFTER, DIAG):
        wq_ref[variant, 0:t, 0:LANES] = q1
        wq_ref[variant, t:2 * t, 0:LANES] = q2
    fast_ok = flag_ref[0] == 1

    def place(r):
        j = qi + r
        after = j < nk
        return (jnp.where(after, j, j - nk), jnp.where(after, AFTER, BEFORE),
                jnp.where(after, r, nk - r))

    def block(r, rows=(0, t), qcols=(0, t)):
        row0, n = rows
        q0, nq = qcols
        jw, variant, dist = (qi, DIAG, 0) if isinstance(r, int) and r == 0 else place(r)
        kt = kf_ref[pl.ds(pl.multiple_of(jw * t + row0, LANES), n), :]
        if nq == t:
            wq = wq_ref[variant]
        else:
            wq = jnp.concatenate([wq_ref[variant, q0:q0 + nq, :], wq_ref[variant, t + q0:t + q0 + nq, :]], axis=0)
        s = lax.dot_general(kt, wq, (((1,), (1,)), ((), ())), preferred_element_type=F32)
        if isinstance(r, int) and r == 0:
            bias = tdiag_ref[row0:row0 + n, q0:q0 + nq]
            return jnp.concatenate([s[:, 0:nq] + bias, s[:, nq:2 * nq] + bias], axis=1), 0.0, jw
        return s, -c * ((dist - 1) * t).astype(F32), jw


    def fixed_reference(near_rows):
        if near_rows is None:
            plan = [((0, t), [(r, (0, t)) for r in range(nk)])]
        else:
            plan = []
            for q0 in range(0, t, BAND_GROUP):
                q1 = q0 + BAND_GROUP
                lo, hi = max(0, q0 - near_rows), min(t, q1 + near_rows)
                visits = [(0, (lo, hi - lo))]
                n_after = min(t, near_rows - (t - q1))
                n_before = min(t, near_rows - q0)
                if n_after > 0:
                    visits.append((1, (0, n_after)))
                if n_before > 0:
                    visits.append((nk - 1, (t - n_before, n_before)))
                plan.append(((q0, BAND_GROUP), visits))
        for (q0, nq), visits in plan:
            m0 = jnp.concatenate([mref_ref[:, q0:q0 + nq], mref_ref[:, t + q0:t + q0 + nq]], axis=1)
            l = None
            slots = []
            first = 0
            for r, rows in visits:
                s, shift, jw = block(r, rows, (q0, nq))
                p = jnp.exp2(s - (m0 - shift))
                ps = jnp.sum(p, axis=0, keepdims=True)
                l = ps if l is None else l + ps
                p_ref[first:first + rows[1], 0:2 * nq] = p.astype(BF16)
                slots.append((first, jw, rows))
                first += rows[1]
            rl = 1.0 / l
            r1 = rl[:, 0:nq].astype(BF16)
            r2 = (lam_ref[:, 0:1] * rl[:, nq:2 * nq]).astype(BF16)
            out = None
            for first, jw, (row0, n) in slots:
                w = p_ref[first:first + n, 0:nq] * r1 - p_ref[first:first + n, nq:2 * nq] * r2
                pv = jnp.dot(vt_ref[jw, 0:HEAD_DIM, row0:row0 + n], w, preferred_element_type=F32)
                out = pv if out is None else out + pv
            out_ref[:, q0:q0 + nq] = out

    for near_rows in sorted(set(head_near_rows), key=lambda v: (v is None, v)):
        in_class = functools.reduce(jnp.logical_or,
                                    [h == hh for hh, v in enumerate(head_near_rows) if v == near_rows])
        pl.when(fast_ok & in_class)(functools.partial(fixed_reference, near_rows))

    @pl.when(jnp.logical_not(fast_ok))
    def _running_max():
        s, _, jw = block(0)
        m_ref[...] = jnp.max(s, axis=0, keepdims=True)
        acc_ref[...] = jnp.dot(vt_ref[jw], jnp.exp2(s - m_ref[...]).astype(BF16),
                               preferred_element_type=F32)

        def fold(r, carry):
            s, shift, jw = block(r)
            m_old = m_ref[...]
            m_new = jnp.maximum(m_old, jnp.max(s, axis=0, keepdims=True) + shift)
            pv = jnp.dot(vt_ref[jw], jnp.exp2(s - (m_new - shift)).astype(BF16),
                         preferred_element_type=F32)
            acc_ref[...] = jnp.exp2(m_old - m_new) * acc_ref[...] + pv
            m_ref[...] = m_new
            return carry

        lax.fori_loop(1, nk, fold, 0)
        rl = 1.0 / acc_ref[HEAD_DIM:HEAD_DIM + 1, :]
        out_ref[...] = (acc_ref[0:HEAD_DIM, 0:t] * rl[:, 0:t]
                        - acc_ref[0:HEAD_DIM, t:2 * t] * (lam_ref[:, 0:1] * rl[:, t:2 * t]))

    o = out_ref[...]
    ms = jnp.mean(o * o, axis=0, keepdims=True)
    o = o * lax.rsqrt(ms + LN_EPS)
    o = o.T * gain_ref[...] * (1.0 - lambda_init)
    o_ref[0] = o.astype(o_ref.dtype)


def _attention(qkv, c, lq1, lk1, lq2, lk2, gain, *, batch, seq, lambda_init, t=1024):
    nh = N_HEADS
    vec = lambda n: pl.BlockSpec((1, n), lambda h, b, i: (0, 0))
    near = tuple(_near_rows(LOG2E * 2.0 ** (-8.0 * (i + 1) / nh), t, seq // t) for i in range(nh))
    return pl.pallas_call(
        functools.partial(_attn_kernel, t=t, lambda_init=lambda_init, head_near_rows=near),
        grid=(nh, batch, seq // t),
        in_specs=[pl.BlockSpec(memory_space=pltpu.SMEM),
                  vec(QK_DIM), vec(QK_DIM), vec(QK_DIM), vec(QK_DIM), vec(HEAD_DIM),
                  pl.BlockSpec((1, FEAT_ROWS, LANES), lambda h, b, i: (h, 0, 0)),
                  pl.BlockSpec((1, seq, HEAD_DIM), lambda h, b, i: (b, 0, h)),
                  pl.BlockSpec((1, seq, HEAD_DIM), lambda h, b, i: (b, 0, nh + h)),
                  pl.BlockSpec((1, seq, HEAD_DIM), lambda h, b, i: (b, 0, 2 * nh + h))],
        out_specs=pl.BlockSpec((1, t, HEAD_DIM), lambda h, b, i: (b, i, h)),
        out_shape=jax.ShapeDtypeStruct((batch, seq, ATTN_WIDTH), BF16),
        scratch_shapes=[pltpu.VMEM((seq, 2 * LANES), BF16),
                        pltpu.VMEM((seq // t, HEAD_DIM + ONES_ROWS, t), BF16),
                        pltpu.VMEM((3, 2 * t, 2 * LANES), BF16),
                        pltpu.VMEM((t, t), F32),
                        pltpu.VMEM((t, t), F32),
                        pltpu.VMEM((1, 2 * t), F32),
                        pltpu.VMEM((1, 2 * t), F32),
                        pltpu.VMEM((1, LANES), F32),
                        pltpu.VMEM((HEAD_DIM + ONES_ROWS, 2 * t), F32),
                        pltpu.VMEM((seq, 2 * t), BF16),
                        pltpu.VMEM((HEAD_DIM, t), F32),
                        pltpu.SMEM((1,), jnp.int32)],
        compiler_params=_params("arbitrary", "arbitrary", "arbitrary"),
    )(c, lq1, lk1, lq2, lk2, gain, _feature_table(c, t), qkv, qkv, qkv)


def _merge_kernel(a_ref, s_ref, x_ref, wa_ref, ws_ref, wg0_ref, wg1_ref, bg0_ref, bg1_ref, *rest):
    n = len(rest) // 2
    o_ref = rest[n]
    x = x_ref[...]
    g0 = jax.nn.sigmoid(jnp.dot(x, wg0_ref[...], preferred_element_type=F32) + bg0_ref[...])
    g1 = jax.nn.sigmoid(jnp.dot(x, wg1_ref[...], preferred_element_type=F32) + bg1_ref[...])
    ya = jnp.dot(a_ref[...], wa_ref[...], preferred_element_type=F32)
    ys = jnp.dot(s_ref[...], ws_ref[...], preferred_element_type=F32)
    o_ref[...] = (g0 * ya + g1 * ys).astype(o_ref.dtype)
    _cast_chunks(rest[:n], rest[n + 1:])


def _merge(attn, sgo, x, w_in, gate_off, b_gate, w_ba, w_bs, cast_later, *, tm=1024, tn=512):
    m, d = x.shape
    nb = d // tn
    j0 = gate_off // tn
    cast_in, cast_out, cast_shapes = _chunk_cast_specs(cast_later, (m // tm) * nb, lambda i, j: i * nb + j)
    out = pl.pallas_call(
        _merge_kernel,
        grid=(m // tm, d // tn),
        in_specs=[pl.BlockSpec((tm, attn.shape[1]), lambda i, j: (i, 0)),
                  pl.BlockSpec((tm, sgo.shape[1]), lambda i, j: (i, 0)),
                  pl.BlockSpec((tm, d), lambda i, j: (i, 0)),
                  pl.BlockSpec((w_ba.shape[0], tn), lambda i, j: (0, j)),
                  pl.BlockSpec((w_bs.shape[0], tn), lambda i, j: (0, j)),
                  pl.BlockSpec((d, tn), lambda i, j: (0, j0 + j)),
                  pl.BlockSpec((d, tn), lambda i, j: (0, j0 + nb + j)),
                  pl.BlockSpec((1, tn), lambda i, j: (0, j)),
                  pl.BlockSpec((1, tn), lambda i, j: (0, nb + j))] + cast_in,
        out_specs=[pl.BlockSpec((tm, tn), lambda i, j: (i, j))] + cast_out,
        out_shape=[jax.ShapeDtypeStruct((m, d), BF16)] + cast_shapes,
        compiler_params=_params("arbitrary", "arbitrary"),
    )(attn, sgo, x, w_ba, w_bs, w_in, w_in, b_gate, b_gate, *cast_later)
    return out[0], out[1:]


def _oproj_kernel(y_ref, w_ref, x_ref, g_ref, b_ref, o_ref, *, alpha):
    for r in range(y_ref.shape[0] // OPROJ_PIECE_ROWS):
        rows = slice(r * OPROJ_PIECE_ROWS, (r + 1) * OPROJ_PIECE_ROWS)
        y = jnp.dot(y_ref[rows, :], w_ref[...], preferred_element_type=F32)
        o_ref[rows, :] = _layer_norm(alpha * x_ref[rows, :] + y, g_ref[...], b_ref[...])


OPROJ_PIECE_ROWS = 256


def _oproj(merged, w_o, x, g, b, *, alpha, tm=1024):
    m, d = x.shape
    return pl.pallas_call(
        functools.partial(_oproj_kernel, alpha=alpha),
        grid=(m // tm,),
        in_specs=[pl.BlockSpec((tm, d), lambda i: (i, 0)),
                  _resident((d, d)),
                  pl.BlockSpec((tm, d), lambda i: (i, 0)),
                  pl.BlockSpec((1, d), lambda i: (0, 0)),
                  pl.BlockSpec((1, d), lambda i: (0, 0))],
        out_specs=pl.BlockSpec((tm, d), lambda i: (i, 0)),
        out_shape=jax.ShapeDtypeStruct((m, d), F32),
        compiler_params=_params("parallel"),
    )(merged, w_o, x, g, b)


def _mlp_kernel(x_ref, wu_ref, wd_ref, g_ref, b_ref, o_ref, xb_ref, *, alpha):
    f = pl.program_id(1)
    nf = pl.num_programs(1)

    def step(first, last):
        for r in range(x_ref.shape[0] // MLP_PIECE_ROWS):
            rows = slice(r * MLP_PIECE_ROWS, (r + 1) * MLP_PIECE_ROWS)
            if first:
                xb_ref[rows, :] = x_ref[rows, :].astype(BF16)
            hid = jnp.dot(xb_ref[rows, :], wu_ref[...], preferred_element_type=F32)
            hid = jnp.square(jnp.maximum(hid, 0.0)).astype(BF16)
            y = jnp.dot(hid, wd_ref[...], preferred_element_type=F32)
            if not first:
                y = o_ref[rows, :] + y
            if last:
                y = _layer_norm(alpha * x_ref[rows, :] + y, g_ref[...], b_ref[...])
            o_ref[rows, :] = y

    pl.when(f == 0)(functools.partial(step, True, False))
    pl.when((f > 0) & (f < nf - 1))(functools.partial(step, False, False))
    pl.when(f == nf - 1)(functools.partial(step, False, True))


MLP_PIECE_ROWS = 256


def _mlp(x, w_up, w_down, g, b, *, alpha, tm=512, tf=1024):
    m, d = x.shape
    dff = w_up.shape[1]
    return pl.pallas_call(
        functools.partial(_mlp_kernel, alpha=alpha),
        grid=(m // tm, dff // tf),
        in_specs=[pl.BlockSpec((tm, d), lambda i, f: (i, 0)),
                  pl.BlockSpec((d, tf), lambda i, f: (0, f)),
                  pl.BlockSpec((tf, d), lambda i, f: (f, 0)),
                  pl.BlockSpec((1, d), lambda i, f: (0, 0)),
                  pl.BlockSpec((1, d), lambda i, f: (0, 0))],
        out_specs=pl.BlockSpec((tm, d), lambda i, f: (i, 0)),
        out_shape=jax.ShapeDtypeStruct((m, d), F32),
        scratch_shapes=[pltpu.VMEM((tm, d), BF16)],
        compiler_params=_params("parallel", "arbitrary"),
    )(x, w_up, w_down, g, b)


def kernel(x, w_in, lambda_q1, lambda_k1, lambda_q2, lambda_k2, attn_subln_g, sg_ln_g, sg_ln_b, sg_w_s, sg_b_s, b_gate, w_branch_attn, w_branch_sg, w_o, ln1_g, ln1_b, w_up, w_down, ln2_g, ln2_b):
    batch, seq, d = x.shape
    depth = w_in.shape[0]
    m = batch * seq
    alpha = (2.0 * depth) ** 0.25
    k_off = ATTN_WIDTH
    u_off = 3 * ATTN_WIDTH
    gate_off = u_off + 2 * SG_WIDTH
    row = lambda p: p.reshape(1, -1).astype(F32)

    i = jnp.arange(1, N_HEADS + 1, dtype=F32)
    slopes = jnp.exp2(-8.0 * i / N_HEADS)
    col_scale = jnp.where(jnp.arange(w_in.shape[2]) < k_off, LOG2E * QK_DIM ** -0.5, 1.0).astype(F32)

    xf = x.reshape(m, d)
    for l in range(depth):
        lambda_init = 0.8 - 0.6 * math.exp(-0.3 * l)
        w_in_b = (w_in[l] * col_scale).astype(BF16)
        qkv, xb, (w_ba, w_bs, w_o_b) = _proj_cast(xf, w_in_b, u_off,
                                                 [w_branch_attn[l], w_branch_sg[l], w_o[l]])
        gd = SG_WIDTH // SG_GROUPS
        bs_full = jnp.repeat(sg_b_s[l].T.astype(F32), gd, axis=1)
        sgo = _sg_branch(xb, w_in_b, u_off, row(sg_ln_g[l]), row(sg_ln_b[l]), sg_w_s[l].astype(BF16), bs_full)

        attn = _attention(qkv.reshape(batch, seq, 3 * ATTN_WIDTH), LOG2E * slopes,
                          row(lambda_q1[l]), row(lambda_k1[l]), row(lambda_q2[l]), row(lambda_k2[l]),
                          row(attn_subln_g[l]), batch=batch, seq=seq, lambda_init=lambda_init)

        merged, (w_up_b, w_down_b) = _merge(attn.reshape(m, ATTN_WIDTH), sgo, xb, w_in_b, gate_off,
                                            row(b_gate[l]), w_ba, w_bs, [w_up[l], w_down[l]])
        xf = _oproj(merged, w_o_b, xf, row(ln1_g[l]), row(ln1_b[l]), alpha=alpha)
        xf = _mlp(xf, w_up_b, w_down_b, row(ln2_g[l]), row(ln2_b[l]), alpha=alpha)
    return xf.reshape(batch, seq, d)
```

```python
import functools
import math

import jax
import jax.numpy as jnp
from jax import lax
from jax.experimental import pallas as pl
from jax.experimental.pallas import tpu as pltpu

F32 = jnp.float32
BF16 = jnp.bfloat16

N_HEADS = 8
QK_DIM = 64
HEAD_DIM = 2 * QK_DIM
ATTN_WIDTH = N_HEADS * HEAD_DIM
SG_WIDTH = 1024
SG_GROUPS = 8
SG_CHUNK = 128
LN_EPS = 1e-5
INV_SQRT2 = 0.7071067811865476
LOG2E = 1.4426950408889634
LANES = 128

VMEM_LIMIT_BYTES = 60 * 1024 * 1024


def _params(*sem):
    return pltpu.CompilerParams(dimension_semantics=sem, vmem_limit_bytes=VMEM_LIMIT_BYTES)


def _layer_norm(y, g, b):
    mu = jnp.mean(y, axis=-1, keepdims=True)
    d = y - mu
    var = jnp.mean(d * d, axis=-1, keepdims=True)
    return d * lax.rsqrt(var + LN_EPS) * g + b


def _proj_cast_kernel(x_ref, w_ref, *rest, n_cast, n_colblocks):
    cast_in, col_in = rest[:n_cast], rest[n_cast:n_cast + n_colblocks]
    o_ref, xb_ref = rest[n_cast + n_colblocks:n_cast + n_colblocks + 2]
    cast_out, cols_out = rest[n_cast + n_colblocks + 2:-1], rest[-1]
    xb = x_ref[...].astype(BF16)
    xb_ref[...] = xb
    o_ref[...] = jnp.dot(xb, w_ref[...], preferred_element_type=F32).astype(o_ref.dtype)
    _cast_chunks(cast_in, cast_out)
    for j, src in enumerate(col_in):
        width = src.shape[1]
        cols_out[:, j * width:(j + 1) * width] = src[...].astype(BF16)


def _resident(shape, index=None):
    index = (0,) * len(shape) if index is None else index
    return pl.BlockSpec(shape, lambda i: index, pipeline_mode=pl.Buffered(1))


def _chunk_cast_specs(arrays, nsteps, step_of):
    ins, shapes = [], []
    for a in arrays:
        ins.append(pl.BlockSpec((a.shape[0] // nsteps, a.shape[1]), lambda *g: (step_of(*g), 0)))
        shapes.append(jax.ShapeDtypeStruct(a.shape, BF16))
    return ins, list(ins), shapes


def _cast_chunks(src_refs, dst_refs):
    for src, dst in zip(src_refs, dst_refs, strict=True):
        dst[...] = src[...].astype(BF16)


def _proj_cast(x, w, w_full, cast_later, *, tm=512, col_block=1024):
    m, k = x.shape
    n = w.shape[1]
    nsteps = m // tm
    cast_in, cast_out, cast_shapes = _chunk_cast_specs(cast_later, nsteps, lambda i: i)
    rest = w_full.shape[1] - n
    j0, ncb = n // col_block, rest // col_block
    col_in = [pl.BlockSpec((k // nsteps, col_block), functools.partial(lambda i, j: (i, j), j=j0 + j))
              for j in range(ncb)]
    out = pl.pallas_call(
        functools.partial(_proj_cast_kernel, n_cast=len(cast_later), n_colblocks=ncb),
        grid=(nsteps,),
        in_specs=[pl.BlockSpec((tm, k), lambda i: (i, 0)), _resident((k, n))] + cast_in + col_in,
        out_specs=[pl.BlockSpec((tm, n), lambda i: (i, 0)),
                   pl.BlockSpec((tm, k), lambda i: (i, 0))] + cast_out
                  + [pl.BlockSpec((k // nsteps, rest), lambda i: (i, 0))],
        out_shape=[jax.ShapeDtypeStruct((m, n), BF16),
                   jax.ShapeDtypeStruct((m, k), BF16)] + cast_shapes
                  + [jax.ShapeDtypeStruct((k, rest), BF16)],
        compiler_params=_params("arbitrary"),
    )(x, w, *cast_later, *([w_full] * ncb))
    return out[0], out[1], out[2:-1], out[-1]


SG_PIECE_ROWS = 256


def _sg_branch_kernel(x_ref, wu_ref, wv_ref, g_ref, b_ref, ws_ref, bs_ref, o_ref, u_ref, vn_ref):
    gelu = lambda y: 0.5 * y * (1.0 + lax.erf(y * INV_SQRT2))
    pieces = x_ref.shape[0] // SG_PIECE_ROWS
    chunks = SG_PIECE_ROWS // SG_CHUNK
    gd = SG_WIDTH // SG_GROUPS
    for piece in range(pieces):
        rows = slice(piece * SG_PIECE_ROWS, (piece + 1) * SG_PIECE_ROWS)
        u_ref[rows, :] = gelu(jnp.dot(x_ref[rows, :], wu_ref[...], preferred_element_type=F32))
        v = gelu(jnp.dot(x_ref[rows, :], wv_ref[...], preferred_element_type=F32))
        vn_ref[rows, :] = _layer_norm(v, g_ref[...], b_ref[...]).astype(BF16)
    for piece in range(pieces):
        r0 = piece * SG_PIECE_ROWS
        for g in range(SG_GROUPS):
            cols = slice(g * gd, (g + 1) * gd)
            vg = jnp.concatenate([vn_ref[r0 + c * SG_CHUNK:r0 + (c + 1) * SG_CHUNK, cols]
                                  for c in range(chunks)], axis=1)
            mixed = jnp.dot(ws_ref[g], vg, preferred_element_type=F32)
            for c in range(chunks):
                mx = mixed[:, c * gd:(c + 1) * gd] + bs_ref[:, cols]
                rows = slice(r0 + c * SG_CHUNK, r0 + (c + 1) * SG_CHUNK)
                o_ref[rows, cols] = (u_ref[rows, cols] * mx).astype(o_ref.dtype)


def _sg_branch(x, w_in, u_off, ln_g, ln_b, w_s, bs_full, *, tm=512):
    m, k = x.shape
    ju = u_off // SG_WIDTH
    return pl.pallas_call(
        _sg_branch_kernel,
        grid=(m // tm,),
        in_specs=[pl.BlockSpec((tm, k), lambda i: (i, 0)),
                  _resident((k, SG_WIDTH), (0, ju)),
                  _resident((k, SG_WIDTH), (0, ju + 1)),
                  pl.BlockSpec((1, SG_WIDTH), lambda i: (0, 0)),
                  pl.BlockSpec((1, SG_WIDTH), lambda i: (0, 0)),
                  pl.BlockSpec(w_s.shape, lambda i: (0, 0, 0)),
                  pl.BlockSpec(bs_full.shape, lambda i: (0, 0))],
        out_specs=pl.BlockSpec((tm, SG_WIDTH), lambda i: (i, 0)),
        out_shape=jax.ShapeDtypeStruct((m, SG_WIDTH), BF16),
        scratch_shapes=[pltpu.VMEM((tm, SG_WIDTH), F32), pltpu.VMEM((tm, SG_WIDTH), BF16)],
        compiler_params=_params("parallel"),
    )(x, w_in, w_in, ln_g, ln_b, w_s, bs_full)


FEAT_ROWS = 16
ONES_ROWS = 16
BAND_GROUP = 256
BEFORE, AFTER, DIAG = 0, 1, 2
NORM_SLACK = 1.02
BOUND_LIMIT = 64.0
HEADROOM = 29.0
ZERO_EXPONENT = 155.0


def _pos_lo_bits(t):
    return max(0, t.bit_length() - 9)


def _feature_table(c, t):
    radix = float(1 << _pos_lo_bits(t))
    c1 = c.astype(BF16).astype(F32)
    c2 = (c - c1).astype(BF16).astype(F32)
    c3 = (c - c1 - c2).astype(BF16).astype(F32)
    nh = c.shape[0]
    z = jnp.zeros((nh, LANES), F32)
    ind = lambda lo: jnp.zeros((nh, LANES), F32).at[:, lo:lo + 3].set(1.0)
    pieces = jnp.stack([radix * c1, radix * c2, radix * c3, c1, c2, c3], axis=1)
    at = lambda lo: z.at[:, lo:lo + 6].set(pieces)
    rows = [ind(0), ind(3), ind(6), ind(9), at(12),
            at(6), at(0), ind(12), ind(15)]
    return jnp.stack(rows + [z] * (FEAT_ROWS - len(rows)), axis=1)


def _near_rows(c, t, nk):
    rows = -(-int(math.ceil((HEADROOM + ZERO_EXPONENT) / c - 1.0)) // LANES) * LANES
    return rows if rows <= t and nk >= 3 else None


def _max_sq_norms(x_ref, groups=8):
    n = x_ref.shape[1] // groups
    g = jnp.square(x_ref[0, 0:n, :].astype(F32))
    for i in range(1, groups):
        g = jnp.maximum(g, jnp.square(x_ref[0, i * n:(i + 1) * n, :].astype(F32)))
    lane = lax.broadcasted_iota(jnp.int32, g.shape, 1)
    out = []
    for mask in (lane < QK_DIM, lane >= QK_DIM):
        n2 = jnp.sum(jnp.where(mask, g, 0.0), axis=1, keepdims=True)
        out.append(jnp.max(n2, axis=0, keepdims=True))
    return out


def _attn_kernel(c_ref, lq1_ref, lk1_ref, lq2_ref, lk2_ref, gain_ref, tab_ref,
                 q_ref, k_ref, v_ref, o_ref,
                 kf_ref, vt_ref, wq_ref, tbase_ref, tdiag_ref, mref_ref, m_ref, lam_ref, acc_ref, p_ref,
                 out_ref, flag_ref,
                 *, t, lambda_init, head_near_rows):
    h = pl.program_id(0)
    bi = pl.program_id(1)
    qi = pl.program_id(2)
    nk = k_ref.shape[1] // t
    c = c_ref[h]

    @pl.when((h == 0) & (bi == 0) & (qi == 0))
    def _once():
        key = lax.broadcasted_iota(jnp.int32, (t, t), 0)
        qry = lax.broadcasted_iota(jnp.int32, (t, t), 1)
        tbase_ref[...] = jnp.abs(key - qry).astype(F32)
        lam = (jnp.exp(jnp.sum(lq1_ref[...] * lk1_ref[...], axis=-1, keepdims=True))
               - jnp.exp(jnp.sum(lq2_ref[...] * lk2_ref[...], axis=-1, keepdims=True))
               + lambda_init)
        lam_ref[...] = jnp.broadcast_to(lam, lam_ref.shape)

    @pl.when((bi == 0) & (qi == 0))
    def _per_head():
        tdiag_ref[...] = -c * tbase_ref[...]
        pos = lax.broadcasted_iota(jnp.int32, (t, LANES), 0)
        lo_bits = _pos_lo_bits(t)
        split = lambda x: ((x >> lo_bits).astype(F32), (x & ((1 << lo_bits) - 1)).astype(F32))
        hi, lo = split(pos)
        rhi, rlo = split(t - pos)
        tab = tab_ref[0]
        kfeat = (hi * tab[0:1] + lo * tab[1:2] + rhi * tab[2:3] + rlo * tab[3:4] + tab[4:5]).astype(BF16)
        q_before = -(tab[5:6] + hi * tab[7:8] + lo * tab[8:9])
        q_after = -(tab[6:7] + rhi * tab[7:8] + rlo * tab[8:9])
        for j in range(nk):
            kf_ref[j * t:(j + 1) * t, LANES:2 * LANES] = kfeat
            vt_ref[j, HEAD_DIM:, :] = jnp.ones((ONES_ROWS, t), BF16)
        for half in range(2):
            rows = slice(half * t, (half + 1) * t)
            wq_ref[BEFORE, rows, LANES:2 * LANES] = q_before.astype(BF16)
            wq_ref[AFTER, rows, LANES:2 * LANES] = q_after.astype(BF16)
            wq_ref[DIAG, rows, LANES:2 * LANES] = jnp.zeros((t, LANES), BF16)

    @pl.when(qi == 0)
    def _per_batch_and_head():
        kf_ref[:, 0:LANES] = k_ref[0]
        for j in range(nk):
            vt_ref[j, 0:HEAD_DIM, :] = v_ref[0, j * t:(j + 1) * t, :].astype(F32).T.astype(BF16)
        qn = _max_sq_norms(q_ref)
        kn = _max_sq_norms(k_ref)
        bounds = [jnp.sqrt(a * b) * NORM_SLACK for a, b in zip(qn, kn)]
        for half in range(2):
            mref_ref[:, half * t:(half + 1) * t] = jnp.broadcast_to(bounds[half] - HEADROOM, (1, t))
        flag_ref[0] = (jnp.max(jnp.maximum(bounds[0], bounds[1])) <= BOUND_LIMIT).astype(jnp.int32)

    q = q_ref[0, pl.ds(pl.multiple_of(qi * t, t), t), :].astype(F32)
    lane = lax.broadcasted_iota(jnp.int32, q.shape, 1)
    q1 = jnp.where(lane < QK_DIM, q, 0.0).astype(BF16)
    q2 = jnp.where(lane >= QK_DIM, q, 0.0).astype(BF16)
    for variant in (BEFORE, AFTER, DIAG):
        wq_ref[variant, 0:t, 0:LANES] = q1
        wq_ref[variant, t:2 * t, 0:LANES] = q2
    fast_ok = flag_ref[0] == 1

    def place(r):
        j = qi + r
        after = j < nk
        return (jnp.where(after, j, j - nk), jnp.where(after, AFTER, BEFORE),
                jnp.where(after, r, nk - r))

    def block(r, rows=(0, t), qcols=(0, t)):
        row0, n = rows
        q0, nq = qcols
        jw, variant, dist = (qi, DIAG, 0) if isinstance(r, int) and r == 0 else place(r)
        kt = kf_ref[pl.ds(pl.multiple_of(jw * t + row0, LANES), n), :]
        if nq == t:
            wq = wq_ref[variant]
        else:
            wq = jnp.concatenate([wq_ref[variant, q0:q0 + nq, :], wq_ref[variant, t + q0:t + q0 + nq, :]], axis=0)
        s = lax.dot_general(kt, wq, (((1,), (1,)), ((), ())), preferred_element_type=F32)
        if isinstance(r, int) and r == 0:
            bias = tdiag_ref[row0:row0 + n, q0:q0 + nq]
            return jnp.concatenate([s[:, 0:nq] + bias, s[:, nq:2 * nq] + bias], axis=1), 0.0, jw
        return s, -c * ((dist - 1) * t).astype(F32), jw


    def fixed_reference(near_rows):
        if near_rows is None:
            plan = [((0, t), [(r, (0, t)) for r in range(nk)])]
        else:
            plan = []
            for q0 in range(0, t, BAND_GROUP):
                q1 = q0 + BAND_GROUP
                lo, hi = max(0, q0 - near_rows), min(t, q1 + near_rows)
                visits = [(0, (lo, hi - lo))]
                n_after = min(t, near_rows - (t - q1))
                n_before = min(t, near_rows - q0)
                if n_after > 0:
                    visits.append((1, (0, n_after)))
                if n_before > 0:
                    visits.append((nk - 1, (t - n_before, n_before)))
                plan.append(((q0, BAND_GROUP), visits))
        for (q0, nq), visits in plan:
            m0 = jnp.concatenate([mref_ref[:, q0:q0 + nq], mref_ref[:, t + q0:t + q0 + nq]], axis=1)
            l = None
            slots = []
            first = 0
            for r, rows in visits:
                s, shift, jw = block(r, rows, (q0, nq))
                p = jnp.exp2(s - (m0 - shift))
                ps = jnp.sum(p, axis=0, keepdims=True)
                l = ps if l is None else l + ps
                p_ref[first:first + rows[1], 0:2 * nq] = p.astype(BF16)
                slots.append((first, jw, rows))
                first += rows[1]
            rl = 1.0 / l
            r1 = rl[:, 0:nq].astype(BF16)
            r2 = (lam_ref[:, 0:1] * rl[:, nq:2 * nq]).astype(BF16)
            out = None
            for first, jw, (row0, n) in slots:
                w = p_ref[first:first + n, 0:nq] * r1 - p_ref[first:first + n, nq:2 * nq] * r2
                pv = jnp.dot(vt_ref[jw, 0:HEAD_DIM, row0:row0 + n], w, preferred_element_type=F32)
                out = pv if out is None else out + pv
            out_ref[:, q0:q0 + nq] = out

    for near_rows in sorted(set(head_near_rows), key=lambda v: (v is None, v)):
        in_class = functools.reduce(jnp.logical_or,
                                    [h == hh for hh, v in enumerate(head_near_rows) if v == near_rows])
        pl.when(fast_ok & in_class)(functools.partial(fixed_reference, near_rows))

    @pl.when(jnp.logical_not(fast_ok))
    def _running_max():
        s, _, jw = block(0)
        m_ref[...] = jnp.max(s, axis=0, keepdims=True)
        acc_ref[...] = jnp.dot(vt_ref[jw], jnp.exp2(s - m_ref[...]).astype(BF16),
                               preferred_element_type=F32)

        def fold(r, carry):
            s, shift, jw = block(r)
            m_old = m_ref[...]
            m_new = jnp.maximum(m_old, jnp.max(s, axis=0, keepdims=True) + shift)
            pv = jnp.dot(vt_ref[jw], jnp.exp2(s - (m_new - shift)).astype(BF16),
                         preferred_element_type=F32)
            acc_ref[...] = jnp.exp2(m_old - m_new) * acc_ref[...] + pv
            m_ref[...] = m_new
            return carry

        lax.fori_loop(1, nk, fold, 0)
        rl = 1.0 / acc_ref[HEAD_DIM:HEAD_DIM + 1, :]
        out_ref[...] = (acc_ref[0:HEAD_DIM, 0:t] * rl[:, 0:t]
                        - acc_ref[0:HEAD_DIM, t:2 * t] * (lam_ref[:, 0:1] * rl[:, t:2 * t]))

    o = out_ref[...]
    ms = jnp.mean(o * o, axis=0, keepdims=True)
    o = o * lax.rsqrt(ms + LN_EPS)
    o = o.T * gain_ref[...] * (1.0 - lambda_init)
    o_ref[0] = o.astype(o_ref.dtype)


def _attention(qkv, c, lq1, lk1, lq2, lk2, gain, *, batch, seq, lambda_init, t=1024):
    nh = N_HEADS
    vec = lambda n: pl.BlockSpec((1, n), lambda h, b, i: (0, 0))
    near = tuple(_near_rows(LOG2E * 2.0 ** (-8.0 * (i + 1) / nh), t, seq // t) for i in range(nh))
    return pl.pallas_call(
        functools.partial(_attn_kernel, t=t, lambda_init=lambda_init, head_near_rows=near),
        grid=(nh, batch, seq // t),
        in_specs=[pl.BlockSpec(memory_space=pltpu.SMEM),
                  vec(QK_DIM), vec(QK_DIM), vec(QK_DIM), vec(QK_DIM), vec(HEAD_DIM),
                  pl.BlockSpec((1, FEAT_ROWS, LANES), lambda h, b, i: (h, 0, 0)),
                  pl.BlockSpec((1, seq, HEAD_DIM), lambda h, b, i: (b, 0, h)),
                  pl.BlockSpec((1, seq, HEAD_DIM), lambda h, b, i: (b, 0, nh + h)),
                  pl.BlockSpec((1, seq, HEAD_DIM), lambda h, b, i: (b, 0, 2 * nh + h))],
        out_specs=pl.BlockSpec((1, t, HEAD_DIM), lambda h, b, i: (b, i, h)),
        out_shape=jax.ShapeDtypeStruct((batch, seq, ATTN_WIDTH), BF16),
        scratch_shapes=[pltpu.VMEM((seq, 2 * LANES), BF16),
                        pltpu.VMEM((seq // t, HEAD_DIM + ONES_ROWS, t), BF16),
                        pltpu.VMEM((3, 2 * t, 2 * LANES), BF16),
                        pltpu.VMEM((t, t), F32),
                        pltpu.VMEM((t, t), F32),
                        pltpu.VMEM((1, 2 * t), F32),
                        pltpu.VMEM((1, 2 * t), F32),
                        pltpu.VMEM((1, LANES), F32),
                        pltpu.VMEM((HEAD_DIM + ONES_ROWS, 2 * t), F32),
                        pltpu.VMEM((seq, 2 * t), BF16),
                        pltpu.VMEM((HEAD_DIM, t), F32),
                        pltpu.SMEM((1,), jnp.int32)],
        compiler_params=_params("arbitrary", "arbitrary", "arbitrary"),
    )(c, lq1, lk1, lq2, lk2, gain, _feature_table(c, t), qkv, qkv, qkv)


def _merge_kernel(a_ref, s_ref, x_ref, wa_ref, ws_ref, wg0_ref, wg1_ref, bg0_ref, bg1_ref, *rest):
    n = len(rest) // 2
    o_ref = rest[n]
    x = x_ref[...]
    g0 = jax.nn.sigmoid(jnp.dot(x, wg0_ref[...], preferred_element_type=F32) + bg0_ref[...])
    g1 = jax.nn.sigmoid(jnp.dot(x, wg1_ref[...], preferred_element_type=F32) + bg1_ref[...])
    ya = jnp.dot(a_ref[...], wa_ref[...], preferred_element_type=F32)
    ys = jnp.dot(s_ref[...], ws_ref[...], preferred_element_type=F32)
    o_ref[...] = (g0 * ya + g1 * ys).astype(o_ref.dtype)
    _cast_chunks(rest[:n], rest[n + 1:])


def _merge(attn, sgo, x, w_in, gate_off, b_gate, w_ba, w_bs, cast_later, *, tm=1024, tn=512):
    m, d = x.shape
    nb = d // tn
    j0 = gate_off // tn
    cast_in, cast_out, cast_shapes = _chunk_cast_specs(cast_later, (m // tm) * nb, lambda i, j: i * nb + j)
    out = pl.pallas_call(
        _merge_kernel,
        grid=(m // tm, d // tn),
        in_specs=[pl.BlockSpec((tm, attn.shape[1]), lambda i, j: (i, 0)),
                  pl.BlockSpec((tm, sgo.shape[1]), lambda i, j: (i, 0)),
                  pl.BlockSpec((tm, d), lambda i, j: (i, 0)),
                  pl.BlockSpec((w_ba.shape[0], tn), lambda i, j: (0, j)),
                  pl.BlockSpec((w_bs.shape[0], tn), lambda i, j: (0, j)),
                  pl.BlockSpec((d, tn), lambda i, j: (0, j0 + j)),
                  pl.BlockSpec((d, tn), lambda i, j: (0, j0 + nb + j)),
                  pl.BlockSpec((1, tn), lambda i, j: (0, j)),
                  pl.BlockSpec((1, tn), lambda i, j: (0, nb + j))] + cast_in,
        out_specs=[pl.BlockSpec((tm, tn), lambda i, j: (i, j))] + cast_out,
        out_shape=[jax.ShapeDtypeStruct((m, d), BF16)] + cast_shapes,
        compiler_params=_params("arbitrary", "arbitrary"),
    )(attn, sgo, x, w_ba, w_bs, w_in, w_in, b_gate, b_gate, *cast_later)
    return out[0], out[1:]


def _oproj_kernel(y_ref, w_ref, x_ref, g_ref, b_ref, o_ref, *, alpha):
    for r in range(y_ref.shape[0] // OPROJ_PIECE_ROWS):
        rows = slice(r * OPROJ_PIECE_ROWS, (r + 1) * OPROJ_PIECE_ROWS)
        y = jnp.dot(y_ref[rows, :], w_ref[...], preferred_element_type=F32)
        o_ref[rows, :] = _layer_norm(alpha * x_ref[rows, :] + y, g_ref[...], b_ref[...])


OPROJ_PIECE_ROWS = 256


def _oproj(merged, w_o, x, g, b, *, alpha, tm=1024):
    m, d = x.shape
    return pl.pallas_call(
        functools.partial(_oproj_kernel, alpha=alpha),
        grid=(m // tm,),
        in_specs=[pl.BlockSpec((tm, d), lambda i: (i, 0)),
                  _resident((d, d)),
                  pl.BlockSpec((tm, d), lambda i: (i, 0)),
                  pl.BlockSpec((1, d), lambda i: (0, 0)),
                  pl.BlockSpec((1, d), lambda i: (0, 0))],
        out_specs=pl.BlockSpec((tm, d), lambda i: (i, 0)),
        out_shape=jax.ShapeDtypeStruct((m, d), F32),
        compiler_params=_params("parallel"),
    )(merged, w_o, x, g, b)


def _mlp_kernel(x_ref, wu_ref, wd_ref, g_ref, b_ref, o_ref, xb_ref, *, alpha):
    f = pl.program_id(1)
    nf = pl.num_programs(1)

    def step(first, last):
        for r in range(x_ref.shape[0] // MLP_PIECE_ROWS):
            rows = slice(r * MLP_PIECE_ROWS, (r + 1) * MLP_PIECE_ROWS)
            if first:
                xb_ref[rows, :] = x_ref[rows, :].astype(BF16)
            hid = jnp.dot(xb_ref[rows, :], wu_ref[...], preferred_element_type=F32)
            hid = jnp.square(jnp.maximum(hid, 0.0)).astype(BF16)
            y = jnp.dot(hid, wd_ref[...], preferred_element_type=F32)
            if not first:
                y = o_ref[rows, :] + y
            if last:
                y = _layer_norm(alpha * x_ref[rows, :] + y, g_ref[...], b_ref[...])
            o_ref[rows, :] = y

    pl.when(f == 0)(functools.partial(step, True, False))
    pl.when((f > 0) & (f < nf - 1))(functools.partial(step, False, False))
    pl.when(f == nf - 1)(functools.partial(step, False, True))


MLP_PIECE_ROWS = 256


def _mlp(x, w_up, w_down, g, b, *, alpha, tm=512, tf=1024):
    m, d = x.shape
    dff = w_up.shape[1]
    return pl.pallas_call(
        functools.partial(_mlp_kernel, alpha=alpha),
        grid=(m // tm, dff // tf),
        in_specs=[pl.BlockSpec((tm, d), lambda i, f: (i, 0)),
                  pl.BlockSpec((d, tf), lambda i, f: (0, f)),
                  pl.BlockSpec((tf, d), lambda i, f: (f, 0)),
                  pl.BlockSpec((1, d), lambda i, f: (0, 0)),
                  pl.BlockSpec((1, d), lambda i, f: (0, 0))],
        out_specs=pl.BlockSpec((tm, d), lambda i, f: (i, 0)),
        out_shape=jax.ShapeDtypeStruct((m, d), F32),
        scratch_shapes=[pltpu.VMEM((tm, d), BF16)],
        compiler_params=_params("parallel", "arbitrary"),
    )(x, w_up, w_down, g, b)


def kernel(x, w_in, lambda_q1, lambda_k1, lambda_q2, lambda_k2, attn_subln_g, sg_ln_g, sg_ln_b, sg_w_s, sg_b_s, b_gate, w_branch_attn, w_branch_sg, w_o, ln1_g, ln1_b, w_up, w_down, ln2_g, ln2_b):
    batch, seq, d = x.shape
    depth = w_in.shape[0]
    m = batch * seq
    alpha = (2.0 * depth) ** 0.25
    k_off = ATTN_WIDTH
    u_off = 3 * ATTN_WIDTH
    gate_off = u_off + 2 * SG_WIDTH
    row = lambda p: p.reshape(1, -1).astype(F32)

    i = jnp.arange(1, N_HEADS + 1, dtype=F32)
    slopes = jnp.exp2(-8.0 * i / N_HEADS)
    col_scale = jnp.where(jnp.arange(u_off) < k_off, LOG2E * QK_DIM ** -0.5, 1.0).astype(F32)

    xf = x.reshape(m, d)
    for l in range(depth):
        lambda_init = 0.8 - 0.6 * math.exp(-0.3 * l)
        w_qkv = (w_in[l][:, :u_off] * col_scale).astype(BF16)
        qkv, xb, (w_ba, w_bs, w_o_b), w_rest = _proj_cast(
            xf, w_qkv, w_in[l], [w_branch_attn[l], w_branch_sg[l], w_o[l]])
        gd = SG_WIDTH // SG_GROUPS
        bs_full = jnp.repeat(sg_b_s[l].T.astype(F32), gd, axis=1)
        sgo = _sg_branch(xb, w_rest, 0, row(sg_ln_g[l]), row(sg_ln_b[l]), sg_w_s[l].astype(BF16), bs_full)

        attn = _attention(qkv.reshape(batch, seq, 3 * ATTN_WIDTH), LOG2E * slopes,
                          row(lambda_q1[l]), row(lambda_k1[l]), row(lambda_q2[l]), row(lambda_k2[l]),
                          row(attn_subln_g[l]), batch=batch, seq=seq, lambda_init=lambda_init)

        merged, (w_up_b, w_down_b) = _merge(attn.reshape(m, ATTN_WIDTH), sgo, xb, w_rest, gate_off - u_off,
                                            row(b_gate[l]), w_ba, w_bs, [w_up[l], w_down[l]])
        xf = _oproj(merged, w_o_b, xf, row(ln1_g[l]), row(ln1_b[l]), alpha=alpha)
        xf = _mlp(xf, w_up_b, w_down_b, row(ln2_g[l]), row(ln2_b[l]), alpha=alpha)
    return xf.reshape(batch, seq, d)
```

```python
import functools
import math

import jax
import jax.numpy as jnp
from jax import lax
from jax.experimental import pallas as pl
from jax.experimental.pallas import tpu as pltpu

F32 = jnp.float32
BF16 = jnp.bfloat16

N_HEADS = 8
QK_DIM = 64
HEAD_DIM = 2 * QK_DIM
ATTN_WIDTH = N_HEADS * HEAD_DIM
SG_WIDTH = 1024
SG_GROUPS = 8
SG_CHUNK = 128
LN_EPS = 1e-5
INV_SQRT2 = 0.7071067811865476
LOG2E = 1.4426950408889634
LANES = 128

VMEM_LIMIT_BYTES = 60 * 1024 * 1024


def _params(*sem):
    return pltpu.CompilerParams(dimension_semantics=sem, vmem_limit_bytes=VMEM_LIMIT_BYTES)


def _layer_norm(y, g, b):
    mu = jnp.mean(y, axis=-1, keepdims=True)
    d = y - mu
    var = jnp.mean(d * d, axis=-1, keepdims=True)
    return d * lax.rsqrt(var + LN_EPS) * g + b


def _proj_cast_kernel(x_ref, w_ref, *rest, n_cast, n_colblocks):
    cast_in, col_in = rest[:n_cast], rest[n_cast:n_cast + n_colblocks]
    o_ref, xb_ref = rest[n_cast + n_colblocks:n_cast + n_colblocks + 2]
    cast_out, cols_out = rest[n_cast + n_colblocks + 2:-1], rest[-1]
    xb = x_ref[...].astype(BF16)
    xb_ref[...] = xb
    o_ref[...] = jnp.dot(xb, w_ref[...], preferred_element_type=F32).astype(o_ref.dtype)
    _cast_chunks(cast_in, cast_out)
    for j, src in enumerate(col_in):
        width = src.shape[1]
        cols_out[:, j * width:(j + 1) * width] = src[...].astype(BF16)


def _resident(shape, index=None):
    index = (0,) * len(shape) if index is None else index
    return pl.BlockSpec(shape, lambda i: index, pipeline_mode=pl.Buffered(1))


def _chunk_cast_specs(arrays, nsteps, step_of):
    ins, shapes = [], []
    for a in arrays:
        ins.append(pl.BlockSpec((a.shape[0] // nsteps, a.shape[1]), lambda *g: (step_of(*g), 0)))
        shapes.append(jax.ShapeDtypeStruct(a.shape, BF16))
    return ins, list(ins), shapes


def _cast_chunks(src_refs, dst_refs):
    for src, dst in zip(src_refs, dst_refs, strict=True):
        dst[...] = src[...].astype(BF16)


def _proj_cast(x, w, w_full, cast_later, *, tm=512, col_block=1024):
    m, k = x.shape
    n = w.shape[1]
    nsteps = m // tm
    cast_in, cast_out, cast_shapes = _chunk_cast_specs(cast_later, nsteps, lambda i: i)
    rest = w_full.shape[1] - n
    j0, ncb = n // col_block, rest // col_block
    col_in = [pl.BlockSpec((k // nsteps, col_block), functools.partial(lambda i, j: (i, j), j=j0 + j))
              for j in range(ncb)]
    out = pl.pallas_call(
        functools.partial(_proj_cast_kernel, n_cast=len(cast_later), n_colblocks=ncb),
        grid=(nsteps,),
        in_specs=[pl.BlockSpec((tm, k), lambda i: (i, 0)), _resident((k, n))] + cast_in + col_in,
        out_specs=[pl.BlockSpec((tm, n), lambda i: (i, 0)),
                   pl.BlockSpec((tm, k), lambda i: (i, 0))] + cast_out
                  + [pl.BlockSpec((k // nsteps, rest), lambda i: (i, 0))],
        out_shape=[jax.ShapeDtypeStruct((m, n), BF16),
                   jax.ShapeDtypeStruct((m, k), BF16)] + cast_shapes
                  + [jax.ShapeDtypeStruct((k, rest), BF16)],
        compiler_params=_params("arbitrary"),
    )(x, w, *cast_later, *([w_full] * ncb))
    return out[0], out[1], out[2:-1], out[-1]


SG_PIECE_ROWS = 256


def _sg_branch_kernel(x_ref, wu_ref, wv_ref, g_ref, b_ref, ws_ref, bs_ref, o_ref, u_ref, vn_ref):
    gelu = lambda y: 0.5 * y * (1.0 + lax.erf(y * INV_SQRT2))
    pieces = x_ref.shape[0] // SG_PIECE_ROWS
    chunks = SG_PIECE_ROWS // SG_CHUNK
    gd = SG_WIDTH // SG_GROUPS
    for piece in range(pieces):
        rows = slice(piece * SG_PIECE_ROWS, (piece + 1) * SG_PIECE_ROWS)
        u_ref[rows, :] = gelu(jnp.dot(x_ref[rows, :], wu_ref[...], preferred_element_type=F32))
        v = gelu(jnp.dot(x_ref[rows, :], wv_ref[...], preferred_element_type=F32))
        vn_ref[rows, :] = _layer_norm(v, g_ref[...], b_ref[...]).astype(BF16)
    for piece in range(pieces):
        r0 = piece * SG_PIECE_ROWS
        for g in range(SG_GROUPS):
            cols = slice(g * gd, (g + 1) * gd)
            vg = jnp.concatenate([vn_ref[r0 + c * SG_CHUNK:r0 + (c + 1) * SG_CHUNK, cols]
                                  for c in range(chunks)], axis=1)
            mixed = jnp.dot(ws_ref[g], vg, preferred_element_type=F32)
            for c in range(chunks):
                mx = mixed[:, c * gd:(c + 1) * gd] + bs_ref[:, cols]
                rows = slice(r0 + c * SG_CHUNK, r0 + (c + 1) * SG_CHUNK)
                o_ref[rows, cols] = (u_ref[rows, cols] * mx).astype(o_ref.dtype)


def _sg_branch(x, w_in, u_off, ln_g, ln_b, w_s, bs_full, *, tm=512):
    m, k = x.shape
    ju = u_off // SG_WIDTH
    return pl.pallas_call(
        _sg_branch_kernel,
        grid=(m // tm,),
        in_specs=[pl.BlockSpec((tm, k), lambda i: (i, 0)),
                  _resident((k, SG_WIDTH), (0, ju)),
                  _resident((k, SG_WIDTH), (0, ju + 1)),
                  pl.BlockSpec((1, SG_WIDTH), lambda i: (0, 0)),
                  pl.BlockSpec((1, SG_WIDTH), lambda i: (0, 0)),
                  pl.BlockSpec(w_s.shape, lambda i: (0, 0, 0)),
                  pl.BlockSpec(bs_full.shape, lambda i: (0, 0))],
        out_specs=pl.BlockSpec((tm, SG_WIDTH), lambda i: (i, 0)),
        out_shape=jax.ShapeDtypeStruct((m, SG_WIDTH), BF16),
        scratch_shapes=[pltpu.VMEM((tm, SG_WIDTH), F32), pltpu.VMEM((tm, SG_WIDTH), BF16)],
        compiler_params=_params("parallel"),
    )(x, w_in, w_in, ln_g, ln_b, w_s, bs_full)


FEAT_ROWS = 16
ONES_ROWS = 16
NORM_FOLD_GROUPS = 8
BAND_GROUP = 256
BEFORE, AFTER, DIAG = 0, 1, 2
NORM_SLACK = 1.02
BOUND_LIMIT = 64.0
HEADROOM = 29.0
ZERO_EXPONENT = 155.0


def _pos_lo_bits(t):
    return max(0, t.bit_length() - 9)


def _feature_table(c, t):
    radix = float(1 << _pos_lo_bits(t))
    c1 = c.astype(BF16).astype(F32)
    c2 = (c - c1).astype(BF16).astype(F32)
    c3 = (c - c1 - c2).astype(BF16).astype(F32)
    nh = c.shape[0]
    z = jnp.zeros((nh, LANES), F32)
    ind = lambda lo: jnp.zeros((nh, LANES), F32).at[:, lo:lo + 3].set(1.0)
    pieces = jnp.stack([radix * c1, radix * c2, radix * c3, c1, c2, c3], axis=1)
    at = lambda lo: z.at[:, lo:lo + 6].set(pieces)
    rows = [ind(0), ind(3), ind(6), ind(9), at(12),
            at(6), at(0), ind(12), ind(15)]
    return jnp.stack(rows + [z] * (FEAT_ROWS - len(rows)), axis=1)


def _near_rows(c, t, nk):
    rows = -(-int(math.ceil((HEADROOM + ZERO_EXPONENT) / c - 1.0)) // LANES) * LANES
    return rows if rows <= t and nk >= 3 else None


def _max_sq_norms(x_ref, groups=NORM_FOLD_GROUPS):
    n = x_ref.shape[1] // groups
    g = jnp.square(x_ref[0, 0:n, :].astype(F32))
    for i in range(1, groups):
        g = jnp.maximum(g, jnp.square(x_ref[0, i * n:(i + 1) * n, :].astype(F32)))
    lane = lax.broadcasted_iota(jnp.int32, g.shape, 1)
    out = []
    for mask in (lane < QK_DIM, lane >= QK_DIM):
        n2 = jnp.sum(jnp.where(mask, g, 0.0), axis=1, keepdims=True)
        out.append(jnp.max(n2, axis=0, keepdims=True))
    return out


def _attn_kernel(c_ref, lq1_ref, lk1_ref, lq2_ref, lk2_ref, gain_ref, tab_ref,
                 q_ref, k_ref, v_ref, o_ref,
                 kf_ref, vt_ref, wq_ref, tbase_ref, tdiag_ref, mref_ref, m_ref, lam_ref, acc_ref, p_ref,
                 out_ref, flag_ref,
                 *, t, lambda_init, head_near_rows):
    h = pl.program_id(0)
    bi = pl.program_id(1)
    qi = pl.program_id(2)
    nk = k_ref.shape[1] // t
    c = c_ref[h]

    @pl.when((h == 0) & (bi == 0) & (qi == 0))
    def _once():
        key = lax.broadcasted_iota(jnp.int32, (t, t), 0)
        qry = lax.broadcasted_iota(jnp.int32, (t, t), 1)
        tbase_ref[...] = jnp.abs(key - qry).astype(F32)
        lam = (jnp.exp(jnp.sum(lq1_ref[...] * lk1_ref[...], axis=-1, keepdims=True))
               - jnp.exp(jnp.sum(lq2_ref[...] * lk2_ref[...], axis=-1, keepdims=True))
               + lambda_init)
        lam_ref[...] = jnp.broadcast_to(lam, lam_ref.shape)

    @pl.when((bi == 0) & (qi == 0))
    def _per_head():
        tdiag_ref[...] = -c * tbase_ref[...]
        pos = lax.broadcasted_iota(jnp.int32, (t, LANES), 0)
        lo_bits = _pos_lo_bits(t)
        split = lambda x: ((x >> lo_bits).astype(F32), (x & ((1 << lo_bits) - 1)).astype(F32))
        hi, lo = split(pos)
        rhi, rlo = split(t - pos)
        tab = tab_ref[0]
        kfeat = (hi * tab[0:1] + lo * tab[1:2] + rhi * tab[2:3] + rlo * tab[3:4] + tab[4:5]).astype(BF16)
        q_before = -(tab[5:6] + hi * tab[7:8] + lo * tab[8:9])
        q_after = -(tab[6:7] + rhi * tab[7:8] + rlo * tab[8:9])
        for j in range(nk):
            kf_ref[j * t:(j + 1) * t, LANES:2 * LANES] = kfeat
            vt_ref[j, HEAD_DIM:, :] = jnp.ones((ONES_ROWS, t), BF16)
        for half in range(2):
            rows = slice(half * t, (half + 1) * t)
            wq_ref[BEFORE, rows, LANES:2 * LANES] = q_before.astype(BF16)
            wq_ref[AFTER, rows, LANES:2 * LANES] = q_after.astype(BF16)
            wq_ref[DIAG, rows, LANES:2 * LANES] = jnp.zeros((t, LANES), BF16)

    @pl.when(qi == 0)
    def _per_batch_and_head():
        kf_ref[:, 0:LANES] = k_ref[0]
        for j in range(nk):
            vt_ref[j, 0:HEAD_DIM, :] = v_ref[0, j * t:(j + 1) * t, :].T
        qn = _max_sq_norms(q_ref)
        kn = _max_sq_norms(k_ref)
        bounds = [jnp.sqrt(a * b) * NORM_SLACK for a, b in zip(qn, kn)]
        for half in range(2):
            mref_ref[:, half * t:(half + 1) * t] = jnp.broadcast_to(bounds[half] - HEADROOM, (1, t))
        flag_ref[0] = (jnp.max(jnp.maximum(bounds[0], bounds[1])) <= BOUND_LIMIT).astype(jnp.int32)

    q = q_ref[0, pl.ds(pl.multiple_of(qi * t, t), t), :].astype(F32)
    lane = lax.broadcasted_iota(jnp.int32, q.shape, 1)
    q1 = jnp.where(lane < QK_DIM, q, 0.0).astype(BF16)
    q2 = jnp.where(lane >= QK_DIM, q, 0.0).astype(BF16)
    for variant in (BEFORE, AFTER, DIAG):
        wq_ref[variant, 0:t, 0:LANES] = q1
        wq_ref[variant, t:2 * t, 0:LANES] = q2
    fast_ok = flag_ref[0] == 1

    def place(r):
        j = qi + r
        after = j < nk
        return (jnp.where(after, j, j - nk), jnp.where(after, AFTER, BEFORE),
                jnp.where(after, r, nk - r))

    def block(r, rows=(0, t), qcols=(0, t)):
        row0, n = rows
        q0, nq = qcols
        jw, variant, dist = (qi, DIAG, 0) if isinstance(r, int) and r == 0 else place(r)
        kt = kf_ref[pl.ds(pl.multiple_of(jw * t + row0, LANES), n), :]
        if nq == t:
            wq = wq_ref[variant]
        else:
            wq = jnp.concatenate([wq_ref[variant, q0:q0 + nq, :], wq_ref[variant, t + q0:t + q0 + nq, :]], axis=0)
        s = lax.dot_general(kt, wq, (((1,), (1,)), ((), ())), preferred_element_type=F32)
        if isinstance(r, int) and r == 0:
            bias = tdiag_ref[row0:row0 + n, q0:q0 + nq]
            return jnp.concatenate([s[:, 0:nq] + bias, s[:, nq:2 * nq] + bias], axis=1), 0.0, jw
        return s, -c * ((dist - 1) * t).astype(F32), jw


    def fixed_reference(near_rows):
        if near_rows is None:
            plan = [((0, t), [(r, (0, t)) for r in range(nk)])]
        else:
            plan = []
            for q0 in range(0, t, BAND_GROUP):
                q1 = q0 + BAND_GROUP
                lo, hi = max(0, q0 - near_rows), min(t, q1 + near_rows)
                visits = [(0, (lo, hi - lo))]
                n_after = min(t, near_rows - (t - q1))
                n_before = min(t, near_rows - q0)
                if n_after > 0:
                    visits.append((1, (0, n_after)))
                if n_before > 0:
                    visits.append((nk - 1, (t - n_before, n_before)))
                plan.append(((q0, BAND_GROUP), visits))
        for (q0, nq), visits in plan:
            m0 = jnp.concatenate([mref_ref[:, q0:q0 + nq], mref_ref[:, t + q0:t + q0 + nq]], axis=1)
            l = None
            slots = []
            first = 0
            for r, rows in visits:
                s, shift, jw = block(r, rows, (q0, nq))
                p = jnp.exp2(s - (m0 - shift))
                ps = jnp.sum(p, axis=0, keepdims=True)
                l = ps if l is None else l + ps
                p_ref[first:first + rows[1], 0:2 * nq] = p.astype(BF16)
                slots.append((first, jw, rows))
                first += rows[1]
            rl = 1.0 / l
            r1 = rl[:, 0:nq].astype(BF16)
            r2 = (lam_ref[:, 0:1] * rl[:, nq:2 * nq]).astype(BF16)
            out = None
            for first, jw, (row0, n) in slots:
                w = p_ref[first:first + n, 0:nq] * r1 - p_ref[first:first + n, nq:2 * nq] * r2
                pv = jnp.dot(vt_ref[jw, 0:HEAD_DIM, row0:row0 + n], w, preferred_element_type=F32)
                out = pv if out is None else out + pv
            out_ref[:, q0:q0 + nq] = out

    for near_rows in sorted(set(head_near_rows), key=lambda v: (v is None, v)):
        in_class = functools.reduce(jnp.logical_or,
                                    [h == hh for hh, v in enumerate(head_near_rows) if v == near_rows])
        pl.when(fast_ok & in_class)(functools.partial(fixed_reference, near_rows))

    @pl.when(jnp.logical_not(fast_ok))
    def _running_max():
        s, _, jw = block(0)
        m_ref[...] = jnp.max(s, axis=0, keepdims=True)
        acc_ref[...] = jnp.dot(vt_ref[jw], jnp.exp2(s - m_ref[...]).astype(BF16),
                               preferred_element_type=F32)

        def fold(r, carry):
            s, shift, jw = block(r)
            m_old = m_ref[...]
            m_new = jnp.maximum(m_old, jnp.max(s, axis=0, keepdims=True) + shift)
            pv = jnp.dot(vt_ref[jw], jnp.exp2(s - (m_new - shift)).astype(BF16),
                         preferred_element_type=F32)
            acc_ref[...] = jnp.exp2(m_old - m_new) * acc_ref[...] + pv
            m_ref[...] = m_new
            return carry

        lax.fori_loop(1, nk, fold, 0)
        rl = 1.0 / acc_ref[HEAD_DIM:HEAD_DIM + 1, :]
        out_ref[...] = (acc_ref[0:HEAD_DIM, 0:t] * rl[:, 0:t]
                        - acc_ref[0:HEAD_DIM, t:2 * t] * (lam_ref[:, 0:1] * rl[:, t:2 * t]))

    o = out_ref[...]
    ms = jnp.mean(o * o, axis=0, keepdims=True)
    o = o * lax.rsqrt(ms + LN_EPS)
    o = o.T * gain_ref[...] * (1.0 - lambda_init)
    o_ref[0] = o.astype(o_ref.dtype)


def _attention(qkv, c, lq1, lk1, lq2, lk2, gain, *, batch, seq, lambda_init, t=1024):
    nh = N_HEADS
    vec = lambda n: pl.BlockSpec((1, n), lambda h, b, i: (0, 0))
    near = tuple(_near_rows(LOG2E * 2.0 ** (-8.0 * (i + 1) / nh), t, seq // t) for i in range(nh))
    return pl.pallas_call(
        functools.partial(_attn_kernel, t=t, lambda_init=lambda_init, head_near_rows=near),
        grid=(nh, batch, seq // t),
        in_specs=[pl.BlockSpec(memory_space=pltpu.SMEM),
                  vec(QK_DIM), vec(QK_DIM), vec(QK_DIM), vec(QK_DIM), vec(HEAD_DIM),
                  pl.BlockSpec((1, FEAT_ROWS, LANES), lambda h, b, i: (h, 0, 0)),
                  pl.BlockSpec((1, seq, HEAD_DIM), lambda h, b, i: (b, 0, h)),
                  pl.BlockSpec((1, seq, HEAD_DIM), lambda h, b, i: (b, 0, nh + h)),
                  pl.BlockSpec((1, seq, HEAD_DIM), lambda h, b, i: (b, 0, 2 * nh + h))],
        out_specs=pl.BlockSpec((1, t, HEAD_DIM), lambda h, b, i: (b, i, h)),
        out_shape=jax.ShapeDtypeStruct((batch, seq, ATTN_WIDTH), BF16),
        scratch_shapes=[pltpu.VMEM((seq, 2 * LANES), BF16),
                        pltpu.VMEM((seq // t, HEAD_DIM + ONES_ROWS, t), BF16),
                        pltpu.VMEM((3, 2 * t, 2 * LANES), BF16),
                        pltpu.VMEM((t, t), F32),
                        pltpu.VMEM((t, t), F32),
                        pltpu.VMEM((1, 2 * t), F32),
                        pltpu.VMEM((1, 2 * t), F32),
                        pltpu.VMEM((1, LANES), F32),
                        pltpu.VMEM((HEAD_DIM + ONES_ROWS, 2 * t), F32),
                        pltpu.VMEM((seq, 2 * t), BF16),
                        pltpu.VMEM((HEAD_DIM, t), F32),
                        pltpu.SMEM((1,), jnp.int32)],
        compiler_params=_params("arbitrary", "arbitrary", "arbitrary"),
    )(c, lq1, lk1, lq2, lk2, gain, _feature_table(c, t), qkv, qkv, qkv)


def _merge_kernel(a_ref, s_ref, x_ref, wa_ref, ws_ref, wg0_ref, wg1_ref, bg0_ref, bg1_ref, *rest):
    n = len(rest) // 2
    o_ref = rest[n]
    x = x_ref[...]
    g0 = jax.nn.sigmoid(jnp.dot(x, wg0_ref[...], preferred_element_type=F32) + bg0_ref[...])
    g1 = jax.nn.sigmoid(jnp.dot(x, wg1_ref[...], preferred_element_type=F32) + bg1_ref[...])
    ya = jnp.dot(a_ref[...], wa_ref[...], preferred_element_type=F32)
    ys = jnp.dot(s_ref[...], ws_ref[...], preferred_element_type=F32)
    o_ref[...] = (g0 * ya + g1 * ys).astype(o_ref.dtype)
    _cast_chunks(rest[:n], rest[n + 1:])


def _merge(attn, sgo, x, w_in, gate_off, b_gate, w_ba, w_bs, cast_later, *, tm=1024, tn=512):
    m, d = x.shape
    nb = d // tn
    j0 = gate_off // tn
    cast_in, cast_out, cast_shapes = _chunk_cast_specs(cast_later, (m // tm) * nb, lambda i, j: i * nb + j)
    out = pl.pallas_call(
        _merge_kernel,
        grid=(m // tm, d // tn),
        in_specs=[pl.BlockSpec((tm, attn.shape[1]), lambda i, j: (i, 0)),
                  pl.BlockSpec((tm, sgo.shape[1]), lambda i, j: (i, 0)),
                  pl.BlockSpec((tm, d), lambda i, j: (i, 0)),
                  pl.BlockSpec((w_ba.shape[0], tn), lambda i, j: (0, j)),
                  pl.BlockSpec((w_bs.shape[0], tn), lambda i, j: (0, j)),
                  pl.BlockSpec((d, tn), lambda i, j: (0, j0 + j)),
                  pl.BlockSpec((d, tn), lambda i, j: (0, j0 + nb + j)),
                  pl.BlockSpec((1, tn), lambda i, j: (0, j)),
                  pl.BlockSpec((1, tn), lambda i, j: (0, nb + j))] + cast_in,
        out_specs=[pl.BlockSpec((tm, tn), lambda i, j: (i, j))] + cast_out,
        out_shape=[jax.ShapeDtypeStruct((m, d), BF16)] + cast_shapes,
        compiler_params=_params("arbitrary", "arbitrary"),
    )(attn, sgo, x, w_ba, w_bs, w_in, w_in, b_gate, b_gate, *cast_later)
    return out[0], out[1:]


def _oproj_kernel(y_ref, w_ref, x_ref, g_ref, b_ref, o_ref, *, alpha):
    for r in range(y_ref.shape[0] // OPROJ_PIECE_ROWS):
        rows = slice(r * OPROJ_PIECE_ROWS, (r + 1) * OPROJ_PIECE_ROWS)
        y = jnp.dot(y_ref[rows, :], w_ref[...], preferred_element_type=F32)
        o_ref[rows, :] = _layer_norm(alpha * x_ref[rows, :] + y, g_ref[...], b_ref[...])


OPROJ_PIECE_ROWS = 256


def _oproj(merged, w_o, x, g, b, *, alpha, tm=1024):
    m, d = x.shape
    return pl.pallas_call(
        functools.partial(_oproj_kernel, alpha=alpha),
        grid=(m // tm,),
        in_specs=[pl.BlockSpec((tm, d), lambda i: (i, 0)),
                  _resident((d, d)),
                  pl.BlockSpec((tm, d), lambda i: (i, 0)),
                  pl.BlockSpec((1, d), lambda i: (0, 0)),
                  pl.BlockSpec((1, d), lambda i: (0, 0))],
        out_specs=pl.BlockSpec((tm, d), lambda i: (i, 0)),
        out_shape=jax.ShapeDtypeStruct((m, d), F32),
        compiler_params=_params("parallel"),
    )(merged, w_o, x, g, b)


def _mlp_kernel(x_ref, wu_ref, wd_ref, g_ref, b_ref, o_ref, xb_ref, *, alpha):
    f = pl.program_id(1)
    nf = pl.num_programs(1)

    def step(first, last):
        for r in range(x_ref.shape[0] // MLP_PIECE_ROWS):
            rows = slice(r * MLP_PIECE_ROWS, (r + 1) * MLP_PIECE_ROWS)
            if first:
                xb_ref[rows, :] = x_ref[rows, :].astype(BF16)
            hid = jnp.dot(xb_ref[rows, :], wu_ref[...], preferred_element_type=F32)
            hid = jnp.square(jnp.maximum(hid, 0.0)).astype(BF16)
            y = jnp.dot(hid, wd_ref[...], preferred_element_type=F32)
            if not first:
                y = o_ref[rows, :] + y
            if last:
                y = _layer_norm(alpha * x_ref[rows, :] + y, g_ref[...], b_ref[...])
            o_ref[rows, :] = y

    pl.when(f == 0)(functools.partial(step, True, False))
    pl.when((f > 0) & (f < nf - 1))(functools.partial(step, False, False))
    pl.when(f == nf - 1)(functools.partial(step, False, True))


MLP_PIECE_ROWS = 256


def _mlp(x, w_up, w_down, g, b, *, alpha, tm=512, tf=1024):
    m, d = x.shape
    dff = w_up.shape[1]
    return pl.pallas_call(
        functools.partial(_mlp_kernel, alpha=alpha),
        grid=(m // tm, dff // tf),
        in_specs=[pl.BlockSpec((tm, d), lambda i, f: (i, 0)),
                  pl.BlockSpec((d, tf), lambda i, f: (0, f)),
                  pl.BlockSpec((tf, d), lambda i, f: (f, 0)),
                  pl.BlockSpec((1, d), lambda i, f: (0, 0)),
                  pl.BlockSpec((1, d), lambda i, f: (0, 0))],
        out_specs=pl.BlockSpec((tm, d), lambda i, f: (i, 0)),
        out_shape=jax.ShapeDtypeStruct((m, d), F32),
        scratch_shapes=[pltpu.VMEM((tm, d), BF16)],
        compiler_params=_params("parallel", "arbitrary"),
    )(x, w_up, w_down, g, b)


def kernel(x, w_in, lambda_q1, lambda_k1, lambda_q2, lambda_k2, attn_subln_g, sg_ln_g, sg_ln_b, sg_w_s, sg_b_s, b_gate, w_branch_attn, w_branch_sg, w_o, ln1_g, ln1_b, w_up, w_down, ln2_g, ln2_b):
    batch, seq, d = x.shape
    depth = w_in.shape[0]
    m = batch * seq
    alpha = (2.0 * depth) ** 0.25
    k_off = ATTN_WIDTH
    u_off = 3 * ATTN_WIDTH
    gate_off = u_off + 2 * SG_WIDTH
    row = lambda p: p.reshape(1, -1).astype(F32)

    i = jnp.arange(1, N_HEADS + 1, dtype=F32)
    slopes = jnp.exp2(-8.0 * i / N_HEADS)
    col_scale = jnp.where(jnp.arange(u_off) < k_off, LOG2E * QK_DIM ** -0.5, 1.0).astype(F32)

    xf = x.reshape(m, d)
    for l in range(depth):
        lambda_init = 0.8 - 0.6 * math.exp(-0.3 * l)
        w_qkv = (w_in[l][:, :u_off] * col_scale).astype(BF16)
        qkv, xb, (w_ba, w_bs, w_o_b), w_rest = _proj_cast(
            xf, w_qkv, w_in[l], [w_branch_attn[l], w_branch_sg[l], w_o[l]])
        gd = SG_WIDTH // SG_GROUPS
        bs_full = jnp.repeat(sg_b_s[l].T.astype(F32), gd, axis=1)
        sgo = _sg_branch(xb, w_rest, 0, row(sg_ln_g[l]), row(sg_ln_b[l]), sg_w_s[l].astype(BF16), bs_full)

        attn = _attention(qkv.reshape(batch, seq, 3 * ATTN_WIDTH), LOG2E * slopes,
                          row(lambda_q1[l]), row(lambda_k1[l]), row(lambda_q2[l]), row(lambda_k2[l]),
                          row(attn_subln_g[l]), batch=batch, seq=seq, lambda_init=lambda_init)

        merged, (w_up_b, w_down_b) = _merge(attn.reshape(m, ATTN_WIDTH), sgo, xb, w_rest, gate_off - u_off,
                                            row(b_gate[l]), w_ba, w_bs, [w_up[l], w_down[l]])
        xf = _oproj(merged, w_o_b, xf, row(ln1_g[l]), row(ln1_b[l]), alpha=alpha)
        xf = _mlp(xf, w_up_b, w_down_b, row(ln2_g[l]), row(ln2_b[l]), alpha=alpha)
    return xf.reshape(batch, seq, d)
```
